```python
import jax, jax.numpy as jnp
from jax import lax
import numpy as np

D_MODEL = 4096
BATCH = 1
SEQ = 8192
DEPTH = 2
DEC_BATCH = 1
DEC_SEQ = 16384
PAST_LEN = 128

GRID_W = 64
N_MIXERS = 2
N_ATTN_LAYERS = (DEPTH + 1) // 2
N_GLA_LAYERS = DEPTH // 2
HEAD_DIM = 128
N_Q_HEADS = D_MODEL // HEAD_DIM
N_KV_HEADS = N_Q_HEADS // 4
GQA_GROUP = N_Q_HEADS // N_KV_HEADS
Q_BLOCK = 128
ROPE_THETA = 10000.0
ROPE_AXIS_DIM = HEAD_DIM // 2
QKV_COLS = D_MODEL + 2 * N_KV_HEADS * HEAD_DIM
GLA_HEADS = 4
GLA_DK_TOTAL = D_MODEL // 2
GLA_DV_TOTAL = D_MODEL
GLA_DK = GLA_DK_TOTAL // GLA_HEADS
GLA_DV = GLA_DV_TOTAL // GLA_HEADS
GLA_GATE_RANK = 16
GLA_GATE_NORM = 16.0
GLA_CHUNK = 64
GLA_IN_COLS = 2 * GLA_DK_TOTAL + 2 * GLA_DV_TOTAL + 2 * GLA_GATE_RANK
D_FF = 256 * ((8 * D_MODEL // 3 + 255) // 256)
FFN_HALF = 0.5
EPS = 1e-6

kernel_name = "hybrid_axial_gqa_bigla_macaron_encoder"


def rmsnorm(x, g):
    xf = x.astype(jnp.float32)
    y = xf * lax.rsqrt(jnp.mean(xf * xf, axis=-1, keepdims=True) + EPS)
    return (y * g.astype(jnp.float32)).astype(x.dtype)


def swiglu(h, w13, w2):
    a, b = jnp.split(h @ w13, 2, axis=-1)
    return (jax.nn.silu(a) * b) @ w2


def axial_rope_tables(seq_len):
    rows = seq_len // GRID_W
    row_ids = jnp.repeat(jnp.arange(rows, dtype=jnp.float32), GRID_W)
    col_ids = jnp.tile(jnp.arange(GRID_W, dtype=jnp.float32), rows)
    inv_freq = ROPE_THETA ** (-jnp.arange(0, ROPE_AXIS_DIM, 2, dtype=jnp.float32) / ROPE_AXIS_DIM)
    ang_r = row_ids[:, None] * inv_freq[None, :]
    ang_c = col_ids[:, None] * inv_freq[None, :]
    return jnp.cos(ang_r), jnp.sin(ang_r), jnp.cos(ang_c), jnp.sin(ang_c)


def rotate_half_axis(x, cos, sin):
    x1, x2 = jnp.split(x, 2, axis=-1)
    c = cos[None, :, None, :]
    s = sin[None, :, None, :]
    return jnp.concatenate([x1 * c - x2 * s, x2 * c + x1 * s], axis=-1)


def apply_axial_rope(x, tables):
    cr, sr, cc, sc = tables
    xf = x.astype(jnp.float32)
    return jnp.concatenate([rotate_half_axis(xf[..., :ROPE_AXIS_DIM], cr, sr),
                            rotate_half_axis(xf[..., ROPE_AXIS_DIM:], cc, sc)], axis=-1)


def attention_mixer(h, w_qkv, q_norm, k_norm, w_o):
    B, S, _ = h.shape
    qkv = h @ w_qkv
    q = qkv[..., :D_MODEL].reshape(B, S, N_Q_HEADS, HEAD_DIM)
    k = qkv[..., D_MODEL:D_MODEL + N_KV_HEADS * HEAD_DIM].reshape(B, S, N_KV_HEADS, HEAD_DIM)
    v = qkv[..., D_MODEL + N_KV_HEADS * HEAD_DIM:].reshape(B, S, N_KV_HEADS, HEAD_DIM)
    tables = axial_rope_tables(S)
    q = (apply_axial_rope(rmsnorm(q, q_norm), tables) * (HEAD_DIM ** -0.5)).astype(h.dtype)
    k = apply_axial_rope(rmsnorm(k, k_norm), tables).astype(h.dtype)
    q = q.reshape(B, S // Q_BLOCK, Q_BLOCK, N_KV_HEADS, GQA_GROUP, HEAD_DIM)
    qb = jnp.moveaxis(q, 1, 0)

    def attend_block(qblk):
        s = jnp.einsum('bqkgd,bskd->bkgqs', qblk, k, preferred_element_type=jnp.float32)
        p = jax.nn.softmax(s, axis=-1).astype(v.dtype)
        return jnp.einsum('bkgqs,bskd->bqkgd', p, v)

    o = lax.map(attend_block, qb)
    o = jnp.moveaxis(o, 0, 1).reshape(B, S, D_MODEL)
    return o @ w_o


def to_chunks(x, n_heads):
    B, S, hd = x.shape
    d = hd // n_heads
    x = x.reshape(B, S, n_heads, d).transpose(0, 2, 1, 3)
    return x.reshape(B, n_heads, S // GLA_CHUNK, GLA_CHUNK, d)


def gla_chunked(q, k, v, g):
    B, H, N, C, dk = q.shape
    dv = v.shape[-1]
    b = jnp.cumsum(g, axis=3)
    b_last = b[:, :, :, -1:, :]
    q_dec = q * jnp.exp(b)
    k_dec = k * jnp.exp(-b)
    k_end = k * jnp.exp(b_last - b)
    mask = jnp.tril(jnp.ones((C, C), dtype=bool))
    att = jnp.where(mask, jnp.einsum('bhncd,bhnsd->bhncs', q_dec, k_dec), 0.0)
    o_intra = jnp.einsum('bhncs,bhnse->bhnce', att, v)
    decay = jnp.exp(b_last[:, :, :, 0, :])

    def step(state, xs):
        qn, kn, vn, dn = xs
        o = jnp.einsum('bhcd,bhde->bhce', qn, state)
        state = state * dn[..., None] + jnp.einsum('bhcd,bhce->bhde', kn, vn)
        return state, o

    xs = (jnp.moveaxis(q_dec, 2, 0), jnp.moveaxis(k_end, 2, 0),
          jnp.moveaxis(v, 2, 0), jnp.moveaxis(decay, 2, 0))
    state0 = jnp.zeros((B, H, dk, dv), jnp.float32)
    _, o_inter = lax.scan(step, state0, xs)
    return o_intra + jnp.moveaxis(o_inter, 0, 2)


def gla_mixer(h, w_in, w_gate_f, b_gate_f, w_gate_b, b_gate_b, head_norm, w_o):
    B, S, _ = h.shape
    proj = h @ w_in
    splits = [GLA_DK_TOTAL, 2 * GLA_DK_TOTAL, 2 * GLA_DK_TOTAL + GLA_DV_TOTAL,
              2 * GLA_DK_TOTAL + 2 * GLA_DV_TOTAL, 2 * GLA_DK_TOTAL + 2 * GLA_DV_TOTAL + GLA_GATE_RANK]
    q, k, v, r, zf, zb = jnp.split(proj, splits, axis=-1)
    q = q.astype(jnp.float32) * (GLA_DK ** -0.5)
    k = k.astype(jnp.float32)
    v = v.astype(jnp.float32)
    gf = jax.nn.log_sigmoid((zf @ w_gate_f + b_gate_f).astype(jnp.float32)) / GLA_GATE_NORM
    gb = jax.nn.log_sigmoid((zb @ w_gate_b + b_gate_b).astype(jnp.float32)) / GLA_GATE_NORM
    o_f = gla_chunked(to_chunks(q, GLA_HEADS), to_chunks(k, GLA_HEADS),
                      to_chunks(v, GLA_HEADS), to_chunks(gf, GLA_HEADS))
    rev = lambda t: t[:, ::-1]
    o_b = gla_chunked(to_chunks(rev(q), GLA_HEADS), to_chunks(rev(k), GLA_HEADS),
                      to_chunks(rev(v), GLA_HEADS), to_chunks(rev(gb), GLA_HEADS))
    o_f = o_f.reshape(B, GLA_HEADS, S, GLA_DV)
    o_b = o_b.reshape(B, GLA_HEADS, S, GLA_DV)[:, :, ::-1]
    o = rmsnorm(o_f + o_b, head_norm)
    o = o.transpose(0, 2, 1, 3).reshape(B, S, GLA_DV_TOTAL).astype(h.dtype)
    return (o * jax.nn.silu(r)) @ w_o


def encoder_trunk(x, ffn_norm, ffn_w13, ffn_w2, mix_norm,
                  attn_w_qkv, attn_q_norm, attn_k_norm, attn_w_o,
                  gla_w_in, gla_w_gate_f, gla_b_gate_f, gla_w_gate_b, gla_b_gate_b,
                  gla_head_norm, gla_w_o, final_norm):
    for i in range(DEPTH):
        x = x + FFN_HALF * swiglu(rmsnorm(x, ffn_norm[i, 0]), ffn_w13[i, 0], ffn_w2[i, 0])
        h = rmsnorm(x, mix_norm[i])
        j = i // N_MIXERS
        if i % N_MIXERS == 0:
            x = x + attention_mixer(h, attn_w_qkv[j], attn_q_norm[j], attn_k_norm[j], attn_w_o[j])
        else:
            x = x + gla_mixer(h, gla_w_in[j], gla_w_gate_f[j], gla_b_gate_f[j],
                              gla_w_gate_b[j], gla_b_gate_b[j], gla_head_norm[j], gla_w_o[j])
        x = x + FFN_HALF * swiglu(rmsnorm(x, ffn_norm[i, 1]), ffn_w13[i, 1], ffn_w2[i, 1])
    return rmsnorm(x, final_norm)


def setup_inputs(seed: int = 0) -> dict:
    key = jax.random.key(seed)
    ks = jax.random.split(key, 20)
    f32 = jnp.float32

    def w(k, shape, fan_in):
        return jax.random.normal(k, shape, f32) * (fan_in ** -0.5)

    def gain(k, shape):
        return 1.0 + 0.05 * jax.random.normal(k, shape, f32)

    return {
        "x_prompt": jax.random.normal(ks[0], (BATCH, SEQ, D_MODEL), f32),
        "x_sample": jax.random.normal(ks[1], (DEC_BATCH, DEC_SEQ, D_MODEL), f32),
        "ffn_norm": gain(ks[2], (DEPTH, 2, D_MODEL)),
        "ffn_w13": w(ks[3], (DEPTH, 2, D_MODEL, 2 * D_FF), D_MODEL),
        "ffn_w2": w(ks[4], (DEPTH, 2, D_FF, D_MODEL), D_FF),
        "mix_norm": gain(ks[5], (DEPTH, D_MODEL)),
        "attn_w_qkv": w(ks[6], (N_ATTN_LAYERS, D_MODEL, QKV_COLS), D_MODEL),
        "attn_q_norm": gain(ks[7], (N_ATTN_LAYERS, HEAD_DIM)),
        "attn_k_norm": gain(ks[8], (N_ATTN_LAYERS, HEAD_DIM)),
        "attn_w_o": w(ks[9], (N_ATTN_LAYERS, D_MODEL, D_MODEL), D_MODEL),
        "gla_w_in": w(ks[10], (N_GLA_LAYERS, D_MODEL, GLA_IN_COLS), D_MODEL),
        "gla_w_gate_f": w(ks[11], (N_GLA_LAYERS, GLA_GATE_RANK, GLA_DK_TOTAL), GLA_GATE_RANK),
        "gla_b_gate_f": 0.1 * jax.random.normal(ks[12], (N_GLA_LAYERS, GLA_DK_TOTAL), f32),
        "gla_w_gate_b": w(ks[13], (N_GLA_LAYERS, GLA_GATE_RANK, GLA_DK_TOTAL), GLA_GATE_RANK),
        "gla_b_gate_b": 0.1 * jax.random.normal(ks[14], (N_GLA_LAYERS, GLA_DK_TOTAL), f32),
        "gla_head_norm": gain(ks[15], (N_GLA_LAYERS, GLA_DV)),
        "gla_w_o": w(ks[16], (N_GLA_LAYERS, GLA_DV_TOTAL, D_MODEL), GLA_DV_TOTAL),
        "final_norm": gain(ks[17], (D_MODEL,)),
    }


def reference(x_prompt, x_sample, ffn_norm, ffn_w13, ffn_w2, mix_norm,
              attn_w_qkv, attn_q_norm, attn_k_norm, attn_w_o,
              gla_w_in, gla_w_gate_f, gla_b_gate_f, gla_w_gate_b, gla_b_gate_b,
              gla_head_norm, gla_w_o, final_norm):
    y_prompt = encoder_trunk(x_prompt, ffn_norm, ffn_w13, ffn_w2, mix_norm,
                             attn_w_qkv, attn_q_norm, attn_k_norm, attn_w_o,
                             gla_w_in, gla_w_gate_f, gla_b_gate_f, gla_w_gate_b, gla_b_gate_b,
                             gla_head_norm, gla_w_o, final_norm)
    y_sample = encoder_trunk(x_sample, ffn_norm, ffn_w13, ffn_w2, mix_norm,
                             attn_w_qkv, attn_q_norm, attn_k_norm, attn_w_o,
                             gla_w_in, gla_w_gate_f, gla_b_gate_f, gla_w_gate_b, gla_b_gate_b,
                             gla_head_norm, gla_w_o, final_norm)
    return (y_prompt, y_sample)
```

```python
import functools

import jax
import jax.numpy as jnp
from jax import lax
from jax.experimental import pallas as pl
from jax.experimental.pallas import tpu as pltpu

GRID_W = 64
ROPE_THETA = 10000.0
GLA_CHUNK = 64
GLA_GATE_NORM = 16.0
FFN_HALF = 0.5
EPS = 1e-6

LANES = 128
V7X_VMEM_LIMIT_BYTES = 56 * 1024 * 1024

F32 = jnp.float32
BF16 = jnp.bfloat16


def _tile(dim, pref, unit=LANES):
    if dim <= pref:
        return dim
    t = (pref // unit) * unit
    while t >= unit:
        if dim % t == 0:
            return t
        t -= unit
    return dim


def _params(*sem):
    return pltpu.CompilerParams(dimension_semantics=sem, vmem_limit_bytes=V7X_VMEM_LIMIT_BYTES)


def _rmsnorm_kernel(x_ref, g_ref, o_ref):
    x = x_ref[...]
    ms = jnp.mean(x * x, axis=-1, keepdims=True)
    o_ref[...] = ((x * lax.rsqrt(ms + EPS)) * g_ref[...]).astype(o_ref.dtype)


def _rmsnorm(x, g, out_dtype):
    m, d = x.shape
    tr = _tile(m, 256, 8)
    return pl.pallas_call(
        _rmsnorm_kernel,
        grid=(m // tr,),
        in_specs=[pl.BlockSpec((tr, d), lambda i: (i, 0)),
                  pl.BlockSpec((1, d), lambda i: (0, 0))],
        out_specs=pl.BlockSpec((tr, d), lambda i: (i, 0)),
        out_shape=jax.ShapeDtypeStruct((m, d), out_dtype),
        compiler_params=_params("parallel"),
        name="rmsnorm",
    )(x, g.reshape(1, d))


def _mm_plain_kernel(a_ref, w_ref, o_ref):
    o_ref[...] = jnp.dot(a_ref[...], w_ref[...], preferred_element_type=F32).astype(o_ref.dtype)


def _matmul(a, w, out_dtype, *, tm=1024, tn=512):
    m, k = a.shape
    n = w.shape[1]
    tm, tn = _tile(m, tm, 8), _tile(n, tn)
    return pl.pallas_call(
        _mm_plain_kernel,
        grid=(m // tm, n // tn),
        in_specs=[pl.BlockSpec((tm, k), lambda i, j: (i, 0)),
                  pl.BlockSpec((k, tn), lambda i, j: (0, j))],
        out_specs=pl.BlockSpec((tm, tn), lambda i, j: (i, j)),
        out_shape=jax.ShapeDtypeStruct((m, n), out_dtype),
        compiler_params=_params("parallel", "arbitrary"),
        name="matmul",
    )(a, w)


def _mm_residual_kernel(a_ref, w_ref, r_ref, o_ref, *, scale):
    acc = jnp.dot(a_ref[...], w_ref[...], preferred_element_type=F32)
    o_ref[...] = r_ref[...] + scale * acc


def _matmul_residual(a, w, res, scale, *, tm, tn):
    m, k = a.shape
    n = w.shape[1]
    tm, tn = _tile(m, tm, 8), _tile(n, tn)
    return pl.pallas_call(
        functools.partial(_mm_residual_kernel, scale=scale),
        grid=(m // tm, n // tn),
        in_specs=[pl.BlockSpec((tm, k), lambda i, j: (i, 0)),
                  pl.BlockSpec((k, tn), lambda i, j: (0, j)),
                  pl.BlockSpec((tm, tn), lambda i, j: (i, j))],
        out_specs=pl.BlockSpec((tm, tn), lambda i, j: (i, j)),
        out_shape=jax.ShapeDtypeStruct((m, n), F32),
        compiler_params=_params("parallel", "arbitrary"),
        name="matmul_residual",
    )(a, w, res)


def _ffn_up_kernel(h_ref, wa_ref, wb_ref, o_ref):
    h = h_ref[...]
    a = jnp.dot(h, wa_ref[...], preferred_element_type=F32)
    b = jnp.dot(h, wb_ref[...], preferred_element_type=F32)
    o_ref[...] = ((a * jax.nn.sigmoid(a)) * b).astype(o_ref.dtype)


def _ffn_up(h, w13):
    m, k = h.shape
    f = w13.shape[1] // 2
    tm, tn = _tile(m, 1024, 8), _tile(f, 256)
    nb = f // tn
    return pl.pallas_call(
        _ffn_up_kernel,
        grid=(m // tm, nb),
        in_specs=[pl.BlockSpec((tm, k), lambda i, j: (i, 0)),
                  pl.BlockSpec((k, tn), lambda i, j: (0, j)),
                  pl.BlockSpec((k, tn), lambda i, j: (0, j + nb))],
        out_specs=pl.BlockSpec((tm, tn), lambda i, j: (i, j)),
        out_shape=jax.ShapeDtypeStruct((m, f), BF16),
        compiler_params=_params("parallel", "arbitrary"),
        name="ffn_up",
    )(h, w13, w13)


def _ffn(x, norm_g, w13, w2):
    h = _rmsnorm(x, norm_g, BF16)
    u = _ffn_up(h, w13)
    return _matmul_residual(u, w2, x, FFN_HALF, tm=512, tn=512)


def _rope_tables(seq_len, head_dim):
    axis = head_dim // 2
    pos = jnp.arange(seq_len, dtype=jnp.int32)
    row_ids = (pos // GRID_W).astype(F32)
    col_ids = (pos % GRID_W).astype(F32)
    inv_freq = ROPE_THETA ** (-jnp.arange(0, axis, 2, dtype=F32) / axis)
    ang_r = row_ids[:, None] * inv_freq[None, :]
    ang_c = col_ids[:, None] * inv_freq[None, :]
    cos = jnp.concatenate([jnp.cos(ang_r), jnp.cos(ang_r), jnp.cos(ang_c), jnp.cos(ang_c)], axis=-1)
    sin = jnp.concatenate([-jnp.sin(ang_r), jnp.sin(ang_r), -jnp.sin(ang_c), jnp.sin(ang_c)], axis=-1)
    return cos, sin


def _qkv_kernel(h_ref, w_ref, gain_ref, cos_ref, sin_ref, o_ref, *, hd, n_q_blocks, n_qk_blocks, q_scale):
    j = pl.program_id(1)
    acc = jnp.dot(h_ref[...], w_ref[...], preferred_element_type=F32)
    heads = acc.shape[1] // hd
    quarter = hd // 4

    def norm_rope(post_scale):
        cos = cos_ref[...]
        sin = sin_ref[...]
        lane = lax.broadcasted_iota(jnp.int32, cos.shape, 1)
        first_half = (lane % (2 * quarter)) < quarter
        for hh in range(heads):
            x = acc[:, hh * hd:(hh + 1) * hd]
            ms = jnp.mean(x * x, axis=-1, keepdims=True)
            y = (x * lax.rsqrt(ms + EPS)) * gain_ref[:, hh * hd:(hh + 1) * hd]
            partner = jnp.where(first_half, pltpu.roll(y, hd - quarter, 1), pltpu.roll(y, quarter, 1))
            out = y * cos + partner * sin
            if post_scale != 1.0:
                out = out * post_scale
            o_ref[:, hh * hd:(hh + 1) * hd] = out.astype(o_ref.dtype)

    @pl.when(j < n_q_blocks)
    def _():
        norm_rope(q_scale)

    @pl.when(jnp.logical_and(j >= n_q_blocks, j < n_qk_blocks))
    def _():
        norm_rope(1.0)

    @pl.when(j >= n_qk_blocks)
    def _():
        o_ref[...] = acc.astype(o_ref.dtype)


def _qkv_proj(h, w_qkv, q_norm, k_norm, cos, sin, d_model, hd):
    m, k = h.shape
    n = w_qkv.shape[1]
    kvd = (n - d_model) // 2
    tm = _tile(m, 1024, 8)
    tn = _tile(kvd, 512)
    assert d_model % tn == 0 and tn % hd == 0
    gain = jnp.concatenate([jnp.tile(q_norm, d_model // hd), jnp.tile(k_norm, kvd // hd),
                            jnp.ones((kvd,), F32)]).reshape(1, n)
    kern = functools.partial(_qkv_kernel, hd=hd, n_q_blocks=d_model // tn,
                             n_qk_blocks=(d_model + kvd) // tn, q_scale=hd ** -0.5)
    return pl.pallas_call(
        kern,
        grid=(m // tm, n // tn),
        in_specs=[pl.BlockSpec((tm, k), lambda i, j: (i, 0)),
                  pl.BlockSpec((k, tn), lambda i, j: (0, j)),
                  pl.BlockSpec((1, tn), lambda i, j: (0, j)),
                  pl.BlockSpec((tm, hd), lambda i, j: (i, 0)),
                  pl.BlockSpec((tm, hd), lambda i, j: (i, 0))],
        out_specs=pl.BlockSpec((tm, tn), lambda i, j: (i, j)),
        out_shape=jax.ShapeDtypeStruct((m, n), BF16),
        compiler_params=_params("parallel", "arbitrary"),
        name="qkv_proj",
    )(h, w_qkv, gain, cos, sin)


def _flash_kernel(q_ref, k_ref, v_ref, o_ref, m_scr, l_scr, acc_scr, *, group, hd, tk):
    tq = q_ref.shape[0]
    nk = k_ref.shape[0] // tk
    q = jnp.concatenate([q_ref[:, g * hd:(g + 1) * hd] for g in range(group)], axis=0)
    m_scr[...] = jnp.full(m_scr.shape, -jnp.inf, F32)
    l_scr[...] = jnp.zeros(l_scr.shape, F32)
    acc_scr[...] = jnp.zeros(acc_scr.shape, F32)

    def body(jj, carry):
        r0 = pl.multiple_of(jj * tk, tk)
        kb = k_ref[pl.ds(r0, tk), :]
        vb = v_ref[pl.ds(r0, tk), :]
        s = lax.dot_general(q, kb, (((1,), (1,)), ((), ())), preferred_element_type=F32)
        m_prev = m_scr[...]
        m_new = jnp.maximum(m_prev, jnp.max(s, axis=-1, keepdims=True))
        alpha = jnp.exp(m_prev - m_new)
        p = jnp.exp(s - m_new)
        l_scr[...] = alpha * l_scr[...] + jnp.sum(p, axis=-1, keepdims=True)
        acc_scr[...] = alpha * acc_scr[...] + jnp.dot(p.astype(BF16), vb, preferred_element_type=F32)
        m_scr[...] = m_new
        return carry

    lax.fori_loop(0, nk, body, 0)
    out = acc_scr[...] / l_scr[...]
    for g in range(group):
        o_ref[:, g * hd:(g + 1) * hd] = out[g * tq:(g + 1) * tq].astype(o_ref.dtype)


def _flash_attention(qkv, row0, seq, d_model, hd):
    n = qkv.shape[1]
    kvd = (n - d_model) // 2
    n_kv = kvd // hd
    group = d_model // kvd
    tq = _tile(seq, 256, 8)
    tk = _tile(seq, 512, 8)
    assert row0 % seq == 0 and row0 % tq == 0
    qb0, sb0 = row0 // tq, row0 // seq
    kc0, vc0 = d_model // hd, (d_model + kvd) // hd
    kern = functools.partial(_flash_kernel, group=group, hd=hd, tk=tk)
    return pl.pallas_call(
        kern,
        grid=(n_kv, seq // tq),
        in_specs=[pl.BlockSpec((tq, group * hd), lambda h, i: (qb0 + i, h)),
                  pl.BlockSpec((seq, hd), lambda h, i: (sb0, kc0 + h)),
                  pl.BlockSpec((seq, hd), lambda h, i: (sb0, vc0 + h))],
        out_specs=pl.BlockSpec((tq, group * hd), lambda h, i: (i, h)),
        out_shape=jax.ShapeDtypeStruct((seq, d_model), BF16),
        scratch_shapes=[pltpu.VMEM((group * tq, 1), F32),
                        pltpu.VMEM((group * tq, 1), F32),
                        pltpu.VMEM((group * tq, hd), F32)],
        compiler_params=_params("parallel", "arbitrary"),
        name="flash_attention",
    )(qkv, qkv, qkv)


def _log_sigmoid(x):
    return jnp.minimum(x, 0.0) - jnp.log1p(jnp.exp(-jnp.abs(x)))


def _gla_chunk(q, k, v, z, wg, bg, state_ref, *, reverse, scale):
    c = q.shape[0]
    dk = q.shape[1]
    row = lax.broadcasted_iota(jnp.int32, (c, c), 0)
    col = lax.broadcasted_iota(jnp.int32, (c, c), 1)
    tri = (row <= col) if reverse else (row >= col)
    ones_tri = tri.astype(BF16)

    pre = jnp.dot(z.astype(BF16), wg, preferred_element_type=F32) + bg
    g = _log_sigmoid(pre) / GLA_GATE_NORM
    g1 = g.astype(BF16)
    r1 = g - g1.astype(F32)
    g2 = r1.astype(BF16)
    g3 = (r1 - g2.astype(F32)).astype(BF16)
    b = (jnp.dot(ones_tri, g1, preferred_element_type=F32)
         + jnp.dot(ones_tri, g2, preferred_element_type=F32)
         + jnp.dot(ones_tri, g3, preferred_element_type=F32))
    b_tot = b[0:1, :] if reverse else b[c - 1:c, :]

    qs = q * scale
    q_dec = (qs * jnp.exp(b)).astype(BF16)
    k_dec = (k * jnp.exp(-b)).astype(BF16)
    k_end = (k * jnp.exp(b_tot - b)).astype(BF16)
    vb = v.astype(BF16)

    att = lax.dot_general(q_dec, k_dec, (((1,), (1,)), ((), ())), preferred_element_type=F32)
    att = jnp.where(tri, att, 0.0).astype(BF16)
    st = state_ref[...]
    o = (jnp.dot(att, vb, preferred_element_type=F32)
         + jnp.dot(q_dec, st.astype(BF16), preferred_element_type=F32))
    decay_col = jnp.exp(jnp.transpose(jnp.broadcast_to(b_tot, (LANES, dk)))[:, 0:1])
    state_ref[...] = st * decay_col + lax.dot_general(
        k_end, vb, (((0,), (0,)), ((), ())), preferred_element_type=F32)
    return o


def _gla_fwd_kernel(q_ref, k_ref, v_ref, z_ref, wg_ref, bg_ref, o_ref, state_ref, *, nchunks, scale):
    @pl.when(pl.program_id(1) == 0)
    def _():
        state_ref[...] = jnp.zeros(state_ref.shape, F32)

    def body(ci, carry):
        r0 = pl.multiple_of(ci * GLA_CHUNK, GLA_CHUNK)
        rows = pl.ds(r0, GLA_CHUNK)
        o = _gla_chunk(q_ref[rows, :], k_ref[rows, :], v_ref[rows, :], z_ref[rows, :],
                       wg_ref[...], bg_ref[...], state_ref, reverse=False, scale=scale)
        o_ref[rows, :] = o
        return carry

    lax.fori_loop(0, nchunks, body, 0)


def _gla_bwd_kernel(q_ref, k_ref, v_ref, z_ref, wg_ref, bg_ref, of_ref, r_ref, hn_ref, o_ref, state_ref,
                    *, nchunks, scale):
    @pl.when(pl.program_id(1) == 0)
    def _():
        state_ref[...] = jnp.zeros(state_ref.shape, F32)

    def body(ci, carry):
        cc = nchunks - 1 - ci
        r0 = pl.multiple_of(cc * GLA_CHUNK, GLA_CHUNK)
        rows = pl.ds(r0, GLA_CHUNK)
        ob = _gla_chunk(q_ref[rows, :], k_ref[rows, :], v_ref[rows, :], z_ref[rows, :],
                        wg_ref[...], bg_ref[...], state_ref, reverse=True, scale=scale)
        o = of_ref[rows, :] + ob
        ms = jnp.mean(o * o, axis=-1, keepdims=True)
        on = (o * lax.rsqrt(ms + EPS)) * hn_ref[...]
        r = r_ref[rows, :]
        o_ref[rows, :] = (on * (r * jax.nn.sigmoid(r))).astype(o_ref.dtype)
        return carry

    lax.fori_loop(0, nchunks, body, 0)


def _gla_direction(proj, z, wg_pad, bg, row0, seq, dk_total, dv_total, heads, *, reverse,
                   o_fwd=None, head_norm=None):
    dk, dv = dk_total // heads, dv_total // heads
    rows = _tile(seq, 256, GLA_CHUNK)
    nchunks = rows // GLA_CHUNK
    nb = seq // rows
    assert row0 % rows == 0
    rb0 = row0 // rows
    kc0 = dk_total // dk
    vc0 = (2 * dk_total) // dv
    rc0 = (2 * dk_total + dv_total) // dv
    if reverse:
        rblk = lambda i: rb0 + nb - 1 - i
        oblk = lambda i: nb - 1 - i
    else:
        rblk = lambda i: rb0 + i
        oblk = lambda i: i
    in_specs = [pl.BlockSpec((rows, dk), lambda h, i: (rblk(i), h)),
                pl.BlockSpec((rows, dk), lambda h, i: (rblk(i), kc0 + h)),
                pl.BlockSpec((rows, dv), lambda h, i: (rblk(i), vc0 + h)),
                pl.BlockSpec((rows, LANES), lambda h, i: (rblk(i), 0)),
                pl.BlockSpec((LANES, dk), lambda h, i: (0, h)),
                pl.BlockSpec((1, dk), lambda h, i: (0, h))]
    args = [proj, proj, proj, z, wg_pad, bg.reshape(1, dk_total)]
    scale = dk ** -0.5
    if reverse:
        in_specs += [pl.BlockSpec((rows, dv), lambda h, i: (oblk(i), h)),
                     pl.BlockSpec((rows, dv), lambda h, i: (rblk(i), rc0 + h)),
                     pl.BlockSpec((1, dv), lambda h, i: (0, 0))]
        args += [o_fwd, proj, head_norm.reshape(1, dv)]
        kern = functools.partial(_gla_bwd_kernel, nchunks=nchunks, scale=scale)
        out_dtype = BF16
    else:
        kern = functools.partial(_gla_fwd_kernel, nchunks=nchunks, scale=scale)
        out_dtype = F32
    return pl.pallas_call(
        kern,
        grid=(heads, nb),
        in_specs=in_specs,
        out_specs=pl.BlockSpec((rows, dv), lambda h, i: (oblk(i), h)),
        out_shape=jax.ShapeDtypeStruct((seq, dv_total), out_dtype),
        scratch_shapes=[pltpu.VMEM((dk, dv), F32)],
        compiler_params=_params("parallel", "arbitrary"),
        name="gla_bwd" if reverse else "gla_fwd",
    )(*args)


def kernel(x_prompt, x_sample, ffn_norm, ffn_w13, ffn_w2, mix_norm, attn_w_qkv, attn_q_norm, attn_k_norm, attn_w_o, gla_w_in, gla_w_gate_f, gla_b_gate_f, gla_w_gate_b, gla_b_gate_b, gla_head_norm, gla_w_o, final_norm):
    d_model = x_prompt.shape[-1]
    hd = attn_q_norm.shape[-1]
    depth = ffn_norm.shape[0]
    dk_total = gla_w_gate_f.shape[-1]
    rank = gla_w_gate_f.shape[1]
    dv_total = d_model
    gla_heads = dv_total // gla_head_norm.shape[-1]
    assert x_prompt.shape[0] == 1 and x_sample.shape[0] == 1 and 2 * rank <= LANES

    named = sorted([("sample", x_sample[0]), ("prompt", x_prompt[0])], key=lambda a: -a[1].shape[0])
    seqs = [s for _, s in named]
    lens = [s.shape[0] for s in seqs]
    starts = [sum(lens[:i]) for i in range(len(lens))]
    x = jnp.concatenate(seqs, axis=0)

    tables = [_rope_tables(n, hd) for n in lens]
    cos = jnp.concatenate([t[0] for t in tables], axis=0)
    sin = jnp.concatenate([t[1] for t in tables], axis=0)

    for i in range(depth):
        x = _ffn(x, ffn_norm[i, 0], ffn_w13[i, 0].astype(BF16), ffn_w2[i, 0].astype(BF16))
        h = _rmsnorm(x, mix_norm[i], BF16)
        j = i // 2
        if i % 2 == 0:
            qkv = _qkv_proj(h, attn_w_qkv[j].astype(BF16), attn_q_norm[j], attn_k_norm[j], cos, sin, d_model, hd)
            o = jnp.concatenate([_flash_attention(qkv, r0, n, d_model, hd) for r0, n in zip(starts, lens)], axis=0)
            x = _matmul_residual(o, attn_w_o[j].astype(BF16), x, 1.0, tm=1024, tn=512)
        else:
            n_main = 2 * dk_total + 2 * dv_total
            w_in = gla_w_in[j]
            proj = _matmul(h, w_in[:, :n_main].astype(BF16), F32)
            w_z = jnp.zeros((d_model, LANES), BF16).at[:, :2 * rank].set(w_in[:, n_main:].astype(BF16))
            z = _matmul(h, w_z, F32, tn=LANES)
            wgf = jnp.zeros((LANES, dk_total), BF16).at[:rank].set(gla_w_gate_f[j].astype(BF16))
            wgb = jnp.zeros((LANES, dk_total), BF16).at[rank:2 * rank].set(gla_w_gate_b[j].astype(BF16))
            outs = []
            for r0, n in zip(starts, lens):
                o_f = _gla_direction(proj, z, wgf, gla_b_gate_f[j], r0, n, dk_total, dv_total, gla_heads,
                                     reverse=False)
                outs.append(_gla_direction(proj, z, wgb, gla_b_gate_b[j], r0, n, dk_total, dv_total, gla_heads,
                                           reverse=True, o_fwd=o_f, head_norm=gla_head_norm[j]))
            o = jnp.concatenate(outs, axis=0)
            x = _matmul_residual(o, gla_w_o[j].astype(BF16), x, 1.0, tm=1024, tn=512)
        x = _ffn(x, ffn_norm[i, 1], ffn_w13[i, 1].astype(BF16), ffn_w2[i, 1].astype(BF16))

    y = _rmsnorm(x, final_norm, F32)
    outs = {name: y[r0:r0 + n][None] for (name, _), r0, n in zip(named, starts, lens)}
    return (outs["prompt"], outs["sample"])
```

```python
import functools

import jax
import jax.numpy as jnp
from jax import lax
from jax.experimental import pallas as pl
from jax.experimental.pallas import tpu as pltpu

GRID_W = 64
ROPE_THETA = 10000.0
GLA_CHUNK = 64
GLA_GATE_NORM = 16.0
FFN_HALF = 0.5
EPS = 1e-6
SOFTMAX_ROWS = 64
FLASH_UNROLL = 4
LOG2_E = 1.4426950408889634

LANES = 128
V7X_VMEM_LIMIT_BYTES = 56 * 1024 * 1024

F32 = jnp.float32
BF16 = jnp.bfloat16


def _tile(dim, pref, unit=LANES):
    if dim <= pref:
        return dim
    t = (pref // unit) * unit
    while t >= unit:
        if dim % t == 0:
            return t
        t -= unit
    return dim


def _params(*sem):
    return pltpu.CompilerParams(dimension_semantics=sem, vmem_limit_bytes=V7X_VMEM_LIMIT_BYTES)


def _rmsnorm_kernel(x_ref, g_ref, o_ref):
    x = x_ref[...]
    ms = jnp.mean(x * x, axis=-1, keepdims=True)
    o_ref[...] = ((x * lax.rsqrt(ms + EPS)) * g_ref[...]).astype(o_ref.dtype)


def _rmsnorm(x, g, out_dtype):
    m, d = x.shape
    tr = _tile(m, 256, 8)
    return pl.pallas_call(
        _rmsnorm_kernel,
        grid=(m // tr,),
        in_specs=[pl.BlockSpec((tr, d), lambda i: (i, 0)),
                  pl.BlockSpec((1, d), lambda i: (0, 0))],
        out_specs=pl.BlockSpec((tr, d), lambda i: (i, 0)),
        out_shape=jax.ShapeDtypeStruct((m, d), out_dtype),
        compiler_params=_params("parallel"),
        name="rmsnorm",
    )(x, g.reshape(1, d))


def _mm_plain_kernel(a_ref, w_ref, o_ref):
    o_ref[...] = jnp.dot(a_ref[...], w_ref[...], preferred_element_type=F32).astype(o_ref.dtype)


def _matmul(a, w, out_dtype, *, tm=1024, tn=512):
    m, k = a.shape
    n = w.shape[1]
    tm, tn = _tile(m, tm, 8), _tile(n, tn)
    return pl.pallas_call(
        _mm_plain_kernel,
        grid=(m // tm, n // tn),
        in_specs=[pl.BlockSpec((tm, k), lambda i, j: (i, 0)),
                  pl.BlockSpec((k, tn), lambda i, j: (0, j))],
        out_specs=pl.BlockSpec((tm, tn), lambda i, j: (i, j)),
        out_shape=jax.ShapeDtypeStruct((m, n), out_dtype),
        compiler_params=_params("parallel", "arbitrary"),
        name="matmul",
    )(a, w)


def _mm_residual_kernel(a_ref, w_ref, r_ref, o_ref, *, scale):
    acc = jnp.dot(a_ref[...], w_ref[...], preferred_element_type=F32)
    o_ref[...] = r_ref[...] + scale * acc


def _matmul_residual(a, w, res, scale, *, tm, tn):
    m, k = a.shape
    n = w.shape[1]
    tm, tn = _tile(m, tm, 8), _tile(n, tn)
    return pl.pallas_call(
        functools.partial(_mm_residual_kernel, scale=scale),
        grid=(m // tm, n // tn),
        in_specs=[pl.BlockSpec((tm, k), lambda i, j: (i, 0)),
                  pl.BlockSpec((k, tn), lambda i, j: (0, j)),
                  pl.BlockSpec((tm, tn), lambda i, j: (i, j))],
        out_specs=pl.BlockSpec((tm, tn), lambda i, j: (i, j)),
        out_shape=jax.ShapeDtypeStruct((m, n), F32),
        compiler_params=_params("parallel", "arbitrary"),
        name="matmul_residual",
    )(a, w, res)


def _ffn_up_kernel(h_ref, wa_ref, wb_ref, o_ref):
    h = h_ref[...]
    a = jnp.dot(h, wa_ref[...], preferred_element_type=F32)
    b = jnp.dot(h, wb_ref[...], preferred_element_type=F32)
    o_ref[...] = ((a * jax.nn.sigmoid(a)) * b).astype(o_ref.dtype)


def _ffn_up(h, w13):
    m, k = h.shape
    f = w13.shape[1] // 2
    tm, tn = _tile(m, 1024, 8), _tile(f, 256)
    nb = f // tn
    return pl.pallas_call(
        _ffn_up_kernel,
        grid=(m // tm, nb),
        in_specs=[pl.BlockSpec((tm, k), lambda i, j: (i, 0)),
                  pl.BlockSpec((k, tn), lambda i, j: (0, j)),
                  pl.BlockSpec((k, tn), lambda i, j: (0, j + nb))],
        out_specs=pl.BlockSpec((tm, tn), lambda i, j: (i, j)),
        out_shape=jax.ShapeDtypeStruct((m, f), BF16),
        compiler_params=_params("parallel", "arbitrary"),
        name="ffn_up",
    )(h, w13, w13)


def _ffn(x, norm_g, w13, w2):
    h = _rmsnorm(x, norm_g, BF16)
    u = _ffn_up(h, w13)
    return _matmul_residual(u, w2, x, FFN_HALF, tm=512, tn=512)


def _rope_tables(seq_len, head_dim):
    axis = head_dim // 2
    pos = jnp.arange(seq_len, dtype=jnp.int32)
    row_ids = (pos // GRID_W).astype(F32)
    col_ids = (pos % GRID_W).astype(F32)
    inv_freq = ROPE_THETA ** (-jnp.arange(0, axis, 2, dtype=F32) / axis)
    ang_r = row_ids[:, None] * inv_freq[None, :]
    ang_c = col_ids[:, None] * inv_freq[None, :]
    cos = jnp.concatenate([jnp.cos(ang_r), jnp.cos(ang_r), jnp.cos(ang_c), jnp.cos(ang_c)], axis=-1)
    sin = jnp.concatenate([-jnp.sin(ang_r), jnp.sin(ang_r), -jnp.sin(ang_c), jnp.sin(ang_c)], axis=-1)
    return cos, sin


def _qkv_kernel(h_ref, w_ref, gain_ref, cos_ref, sin_ref, o_ref, *, hd, n_q_blocks, n_qk_blocks, q_scale):
    j = pl.program_id(1)
    acc = jnp.dot(h_ref[...], w_ref[...], preferred_element_type=F32)
    heads = acc.shape[1] // hd
    quarter = hd // 4

    def norm_rope(post_scale):
        cos = cos_ref[...]
        sin = sin_ref[...]
        lane = lax.broadcasted_iota(jnp.int32, cos.shape, 1)
        first_half = (lane % (2 * quarter)) < quarter
        for hh in range(heads):
            x = acc[:, hh * hd:(hh + 1) * hd]
            ms = jnp.mean(x * x, axis=-1, keepdims=True)
            y = (x * lax.rsqrt(ms + EPS)) * gain_ref[:, hh * hd:(hh + 1) * hd]
            partner = jnp.where(first_half, pltpu.roll(y, hd - quarter, 1), pltpu.roll(y, quarter, 1))
            out = y * cos + partner * sin
            if post_scale != 1.0:
                out = out * post_scale
            o_ref[:, hh * hd:(hh + 1) * hd] = out.astype(o_ref.dtype)

    @pl.when(j < n_q_blocks)
    def _():
        norm_rope(q_scale)

    @pl.when(jnp.logical_and(j >= n_q_blocks, j < n_qk_blocks))
    def _():
        norm_rope(1.0)

    @pl.when(j >= n_qk_blocks)
    def _():
        o_ref[...] = acc.astype(o_ref.dtype)


def _qkv_proj(h, w_qkv, q_norm, k_norm, cos, sin, d_model, hd):
    m, k = h.shape
    n = w_qkv.shape[1]
    kvd = (n - d_model) // 2
    tm = _tile(m, 1024, 8)
    tn = _tile(kvd, 512)
    assert d_model % tn == 0 and tn % hd == 0
    gain = jnp.concatenate([jnp.tile(q_norm, d_model // hd), jnp.tile(k_norm, kvd // hd),
                            jnp.ones((kvd,), F32)]).reshape(1, n)
    kern = functools.partial(_qkv_kernel, hd=hd, n_q_blocks=d_model // tn,
                             n_qk_blocks=(d_model + kvd) // tn, q_scale=hd ** -0.5 * LOG2_E)
    return pl.pallas_call(
        kern,
        grid=(m // tm, n // tn),
        in_specs=[pl.BlockSpec((tm, k), lambda i, j: (i, 0)),
                  pl.BlockSpec((k, tn), lambda i, j: (0, j)),
                  pl.BlockSpec((1, tn), lambda i, j: (0, j)),
                  pl.BlockSpec((tm, hd), lambda i, j: (i, 0)),
                  pl.BlockSpec((tm, hd), lambda i, j: (i, 0))],
        out_specs=pl.BlockSpec((tm, tn), lambda i, j: (i, j)),
        out_shape=jax.ShapeDtypeStruct((m, n), BF16),
        compiler_params=_params("parallel", "arbitrary"),
        name="qkv_proj",
    )(h, w_qkv, gain, cos, sin)


def _flash_kernel(q_ref, k_ref, v_ref, o_ref, m_scr, l_scr, acc_scr, a_scr, p_scr, s_a, s_b,
                  *, group, hd, tk, unroll):
    tq = q_ref.shape[0]
    nk = k_ref.shape[0] // tk
    q = jnp.concatenate([q_ref[:, g * hd:(g + 1) * hd] for g in range(group)], axis=0)
    m_scr[...] = jnp.full(m_scr.shape, -jnp.inf, F32)
    l_scr[...] = jnp.zeros(l_scr.shape, F32)
    acc_scr[...] = jnp.zeros(acc_scr.shape, F32)

    bufs = (s_a, s_b)

    def scores(c, s_out):
        r0 = pl.multiple_of(c * tk, tk)
        s_out[...] = lax.dot_general(q, k_ref[pl.ds(r0, tk), :], (((1,), (1,)), ((), ())),
                                     preferred_element_type=F32)

    def update(c, s_in):
        r0 = pl.multiple_of(c * tk, tk)
        for rb in range(0, group * tq, SOFTMAX_ROWS):
            rows = slice(rb, rb + SOFTMAX_ROWS)
            s = s_in[rows, :]
            m_prev = m_scr[rows, :]
            m_new = jnp.maximum(m_prev, jnp.max(s, axis=-1, keepdims=True))
            alpha = jnp.exp2(m_prev - m_new)
            p = jnp.exp2(s - m_new[:, 0:1])
            psum = p[:, 0:LANES]
            for cc in range(1, tk // LANES):
                psum = psum + p[:, cc * LANES:(cc + 1) * LANES]
            l_scr[rows, :] = alpha * l_scr[rows, :] + psum
            m_scr[rows, :] = m_new
            a_scr[rows, :] = alpha
            p_scr[rows, :] = p.astype(BF16)
        acc_scr[...] = a_scr[...] * acc_scr[...] + jnp.dot(p_scr[...], v_ref[pl.ds(r0, tk), :],
                                                          preferred_element_type=F32)

    scores(0, s_a)

    def steady(jj, carry):
        c0 = unroll * jj
        for u in range(unroll):
            scores(c0 + u + 1, bufs[(u + 1) % 2])
            update(c0 + u, bufs[u % 2])
        return carry

    lax.fori_loop(0, nk // unroll - 1, steady, 0)
    c0 = nk - unroll
    for u in range(unroll):
        if u + 1 < unroll:
            scores(c0 + u + 1, bufs[(u + 1) % 2])
        update(c0 + u, bufs[u % 2])
    out = acc_scr[...] / jnp.sum(l_scr[...], axis=-1, keepdims=True)
    for g in range(group):
        o_ref[:, g * hd:(g + 1) * hd] = out[g * tq:(g + 1) * tq].astype(o_ref.dtype)


def _flash_attention(qkv, row0, seq, d_model, hd):
    n = qkv.shape[1]
    kvd = (n - d_model) // 2
    n_kv = kvd // hd
    group = d_model // kvd
    tq = _tile(seq, 256, 8)
    tk = _tile(seq // 2, 512)
    nk = seq // tk
    unroll = FLASH_UNROLL if nk % FLASH_UNROLL == 0 else 2
    rows = group * tq
    assert hd == LANES and row0 % seq == 0 and row0 % tq == 0 and nk % unroll == 0
    assert rows % SOFTMAX_ROWS == 0
    qb0, sb0 = row0 // tq, row0 // seq
    kc0, vc0 = d_model // hd, (d_model + kvd) // hd
    kern = functools.partial(_flash_kernel, group=group, hd=hd, tk=tk, unroll=unroll)
    return pl.pallas_call(
        kern,
        grid=(n_kv, seq // tq),
        in_specs=[pl.BlockSpec((tq, group * hd), lambda h, i: (qb0 + i, h)),
                  pl.BlockSpec((seq, hd), lambda h, i: (sb0, kc0 + h)),
                  pl.BlockSpec((seq, hd), lambda h, i: (sb0, vc0 + h))],
        out_specs=pl.BlockSpec((tq, group * hd), lambda h, i: (i, h)),
        out_shape=jax.ShapeDtypeStruct((seq, d_model), BF16),
        scratch_shapes=[pltpu.VMEM((rows, LANES), F32),
                        pltpu.VMEM((rows, LANES), F32),
                        pltpu.VMEM((rows, hd), F32),
                        pltpu.VMEM((rows, LANES), F32),
                        pltpu.VMEM((rows, tk), BF16),
                        pltpu.VMEM((rows, tk), F32),
                        pltpu.VMEM((rows, tk), F32)],
        compiler_params=_params("parallel", "arbitrary"),
        name="flash_attention",
    )(qkv, qkv, qkv)


def _log_sigmoid(x):
    return jnp.minimum(x, 0.0) - jnp.log1p(jnp.exp(-jnp.abs(x)))


def _gla_chunk(q, k, v, z, wg, bg, state_ref, *, reverse, scale):
    c = q.shape[0]
    dk = q.shape[1]
    row = lax.broadcasted_iota(jnp.int32, (c, c), 0)
    col = lax.broadcasted_iota(jnp.int32, (c, c), 1)
    tri = (row <= col) if reverse else (row >= col)
    ones_tri = tri.astype(BF16)

    pre = jnp.dot(z.astype(BF16), wg, preferred_element_type=F32) + bg
    g = _log_sigmoid(pre) / GLA_GATE_NORM
    g1 = g.astype(BF16)
    r1 = g - g1.astype(F32)
    g2 = r1.astype(BF16)
    g3 = (r1 - g2.astype(F32)).astype(BF16)
    b = (jnp.dot(ones_tri, g1, preferred_element_type=F32)
         + jnp.dot(ones_tri, g2, preferred_element_type=F32)
         + jnp.dot(ones_tri, g3, preferred_element_type=F32))
    b_tot = b[0:1, :] if reverse else b[c - 1:c, :]

    qs = q * scale
    q_dec = (qs * jnp.exp(b)).astype(BF16)
    k_dec = (k * jnp.exp(-b)).astype(BF16)
    k_end = (k * jnp.exp(b_tot - b)).astype(BF16)
    vb = v.astype(BF16)

    att = lax.dot_general(q_dec, k_dec, (((1,), (1,)), ((), ())), preferred_element_type=F32)
    att = jnp.where(tri, att, 0.0).astype(BF16)
    st = state_ref[...]
    o = (jnp.dot(att, vb, preferred_element_type=F32)
         + jnp.dot(q_dec, st.astype(BF16), preferred_element_type=F32))
    decay_col = jnp.exp(jnp.transpose(jnp.broadcast_to(b_tot, (LANES, dk)))[:, 0:1])
    state_ref[...] = st * decay_col + lax.dot_general(
        k_end, vb, (((0,), (0,)), ((), ())), preferred_element_type=F32)
    return o


def _gla_fwd_kernel(q_ref, k_ref, v_ref, z_ref, wg_ref, bg_ref, o_ref, state_ref, *, nchunks, scale):
    @pl.when(pl.program_id(1) == 0)
    def _():
        state_ref[...] = jnp.zeros(state_ref.shape, F32)

    def body(ci, carry):
        r0 = pl.multiple_of(ci * GLA_CHUNK, GLA_CHUNK)
        rows = pl.ds(r0, GLA_CHUNK)
        o = _gla_chunk(q_ref[rows, :], k_ref[rows, :], v_ref[rows, :], z_ref[rows, :],
                       wg_ref[...], bg_ref[...], state_ref, reverse=False, scale=scale)
        o_ref[rows, :] = o
        return carry

    lax.fori_loop(0, nchunks, body, 0)


def _gla_bwd_kernel(q_ref, k_ref, v_ref, z_ref, wg_ref, bg_ref, of_ref, r_ref, hn_ref, o_ref, state_ref,
                    *, nchunks, scale):
    @pl.when(pl.program_id(1) == 0)
    def _():
        state_ref[...] = jnp.zeros(state_ref.shape, F32)

    def body(ci, carry):
        cc = nchunks - 1 - ci
        r0 = pl.multiple_of(cc * GLA_CHUNK, GLA_CHUNK)
        rows = pl.ds(r0, GLA_CHUNK)
        ob = _gla_chunk(q_ref[rows, :], k_ref[rows, :], v_ref[rows, :], z_ref[rows, :],
                        wg_ref[...], bg_ref[...], state_ref, reverse=True, scale=scale)
        o = of_ref[rows, :] + ob
        ms = jnp.mean(o * o, axis=-1, keepdims=True)
        on = (o * lax.rsqrt(ms + EPS)) * hn_ref[...]
        r = r_ref[rows, :]
        o_ref[rows, :] = (on * (r * jax.nn.sigmoid(r))).astype(o_ref.dtype)
        return carry

    lax.fori_loop(0, nchunks, body, 0)


def _gla_direction(proj, z, wg_pad, bg, row0, seq, dk_total, dv_total, heads, *, reverse,
                   o_fwd=None, head_norm=None):
    dk, dv = dk_total // heads, dv_total // heads
    rows = _tile(seq, 256, GLA_CHUNK)
    nchunks = rows // GLA_CHUNK
    nb = seq // rows
    assert row0 % rows == 0
    rb0 = row0 // rows
    kc0 = dk_total // dk
    vc0 = (2 * dk_total) // dv
    rc0 = (2 * dk_total + dv_total) // dv
    if reverse:
        rblk = lambda i: rb0 + nb - 1 - i
        oblk = lambda i: nb - 1 - i
    else:
        rblk = lambda i: rb0 + i
        oblk = lambda i: i
    in_specs = [pl.BlockSpec((rows, dk), lambda h, i: (rblk(i), h)),
                pl.BlockSpec((rows, dk), lambda h, i: (rblk(i), kc0 + h)),
                pl.BlockSpec((rows, dv), lambda h, i: (rblk(i), vc0 + h)),
                pl.BlockSpec((rows, LANES), lambda h, i: (rblk(i), 0)),
                pl.BlockSpec((LANES, dk), lambda h, i: (0, h)),
                pl.BlockSpec((1, dk), lambda h, i: (0, h))]
    args = [proj, proj, proj, z, wg_pad, bg.reshape(1, dk_total)]
    scale = dk ** -0.5
    if reverse:
        in_specs += [pl.BlockSpec((rows, dv), lambda h, i: (oblk(i), h)),
                     pl.BlockSpec((rows, dv), lambda h, i: (rblk(i), rc0 + h)),
                     pl.BlockSpec((1, dv), lambda h, i: (0, 0))]
        args += [o_fwd, proj, head_norm.reshape(1, dv)]
        kern = functools.partial(_gla_bwd_kernel, nchunks=nchunks, scale=scale)
        out_dtype = BF16
    else:
        kern = functools.partial(_gla_fwd_kernel, nchunks=nchunks, scale=scale)
        out_dtype = F32
    return pl.pallas_call(
        kern,
        grid=(heads, nb),
        in_specs=in_specs,
        out_specs=pl.BlockSpec((rows, dv), lambda h, i: (oblk(i), h)),
        out_shape=jax.ShapeDtypeStruct((seq, dv_total), out_dtype),
        scratch_shapes=[pltpu.VMEM((dk, dv), F32)],
        compiler_params=_params("parallel", "arbitrary"),
        name="gla_bwd" if reverse else "gla_fwd",
    )(*args)


def kernel(x_prompt, x_sample, ffn_norm, ffn_w13, ffn_w2, mix_norm, attn_w_qkv, attn_q_norm, attn_k_norm, attn_w_o, gla_w_in, gla_w_gate_f, gla_b_gate_f, gla_w_gate_b, gla_b_gate_b, gla_head_norm, gla_w_o, final_norm):
    d_model = x_prompt.shape[-1]
    hd = attn_q_norm.shape[-1]
    depth = ffn_norm.shape[0]
    dk_total = gla_w_gate_f.shape[-1]
    rank = gla_w_gate_f.shape[1]
    dv_total = d_model
    gla_heads = dv_total // gla_head_norm.shape[-1]
    assert x_prompt.shape[0] == 1 and x_sample.shape[0] == 1 and 2 * rank <= LANES

    named = sorted([("sample", x_sample[0]), ("prompt", x_prompt[0])], key=lambda a: -a[1].shape[0])
    seqs = [s for _, s in named]
    lens = [s.shape[0] for s in seqs]
    starts = [sum(lens[:i]) for i in range(len(lens))]
    x = jnp.concatenate(seqs, axis=0)

    tables = [_rope_tables(n, hd) for n in lens]
    cos = jnp.concatenate([t[0] for t in tables], axis=0)
    sin = jnp.concatenate([t[1] for t in tables], axis=0)

    for i in range(depth):
        x = _ffn(x, ffn_norm[i, 0], ffn_w13[i, 0].astype(BF16), ffn_w2[i, 0].astype(BF16))
        h = _rmsnorm(x, mix_norm[i], BF16)
        j = i // 2
        if i % 2 == 0:
            qkv = _qkv_proj(h, attn_w_qkv[j].astype(BF16), attn_q_norm[j], attn_k_norm[j], cos, sin, d_model, hd)
            o = jnp.concatenate([_flash_attention(qkv, r0, n, d_model, hd) for r0, n in zip(starts, lens)], axis=0)
            x = _matmul_residual(o, attn_w_o[j].astype(BF16), x, 1.0, tm=1024, tn=512)
        else:
            n_main = 2 * dk_total + 2 * dv_total
            w_in = gla_w_in[j]
            proj = _matmul(h, w_in[:, :n_main].astype(BF16), F32)
            w_z = jnp.zeros((d_model, LANES), BF16).at[:, :2 * rank].set(w_in[:, n_main:].astype(BF16))
            z = _matmul(h, w_z, F32, tn=LANES)
            wgf = jnp.zeros((LANES, dk_total), BF16).at[:rank].set(gla_w_gate_f[j].astype(BF16))
            wgb = jnp.zeros((LANES, dk_total), BF16).at[rank:2 * rank].set(gla_w_gate_b[j].astype(BF16))
            outs = []
            for r0, n in zip(starts, lens):
                o_f = _gla_direction(proj, z, wgf, gla_b_gate_f[j], r0, n, dk_total, dv_total, gla_heads,
                                     reverse=False)
                outs.append(_gla_direction(proj, z, wgb, gla_b_gate_b[j], r0, n, dk_total, dv_total, gla_heads,
                                           reverse=True, o_fwd=o_f, head_norm=gla_head_norm[j]))
            o = jnp.concatenate(outs, axis=0)
            x = _matmul_residual(o, gla_w_o[j].astype(BF16), x, 1.0, tm=1024, tn=512)
        x = _ffn(x, ffn_norm[i, 1], ffn_w13[i, 1].astype(BF16), ffn_w2[i, 1].astype(BF16))

    y = _rmsnorm(x, final_norm, F32)
    outs = {name: y[r0:r0 + n][None] for (name, _), r0, n in zip(named, starts, lens)}
    return (outs["prompt"], outs["sample"])
```

```python
import functools

import jax
import jax.numpy as jnp
from jax import lax
from jax.experimental import pallas as pl
from jax.experimental.pallas import tpu as pltpu

GRID_W = 64
ROPE_THETA = 10000.0
GLA_CHUNK = 64
GLA_GATE_NORM = 16.0
FFN_HALF = 0.5
EPS = 1e-6
SOFTMAX_ROWS = 32
FLASH_UNROLL = 4
GLA_HEADS_PER_STEP = 1
NORM_ROWS = 32
LOG2_E = 1.4426950408889634

LANES = 128
V7X_VMEM_LIMIT_BYTES = 56 * 1024 * 1024

F32 = jnp.float32
BF16 = jnp.bfloat16


def _tile(dim, pref, unit=LANES):
    if dim <= pref:
        return dim
    t = (pref // unit) * unit
    while t >= unit:
        if dim % t == 0:
            return t
        t -= unit
    return dim


def _params(*sem):
    return pltpu.CompilerParams(dimension_semantics=sem, vmem_limit_bytes=V7X_VMEM_LIMIT_BYTES)


def _rmsnorm_kernel(x_ref, g_ref, o_ref):
    x = x_ref[...]
    ms = jnp.mean(x * x, axis=-1, keepdims=True)
    o_ref[...] = ((x * lax.rsqrt(ms + EPS)) * g_ref[...]).astype(o_ref.dtype)


def _rmsnorm(x, g, out_dtype, row0, nrows):
    d = x.shape[1]
    tr = _tile(nrows, 256, 8)
    assert row0 % tr == 0
    rb0 = row0 // tr
    return pl.pallas_call(
        _rmsnorm_kernel,
        grid=(nrows // tr,),
        in_specs=[pl.BlockSpec((tr, d), lambda i: (rb0 + i, 0)),
                  pl.BlockSpec((1, d), lambda i: (0, 0))],
        out_specs=pl.BlockSpec((tr, d), lambda i: (i, 0)),
        out_shape=jax.ShapeDtypeStruct((nrows, d), out_dtype),
        compiler_params=_params("parallel"),
        name="rmsnorm",
    )(x, g.reshape(1, d))


def _fold_lanes(y):
    out = y[:, 0:LANES]
    for c in range(1, y.shape[1] // LANES):
        out = out + y[:, c * LANES:(c + 1) * LANES]
    return out


def _normalize_rows(x_ref, ssq_ref, g_ref, h_scr, r_scr):
    d = x_ref.shape[1]
    ms = jnp.sum(ssq_ref[...], axis=-1, keepdims=True) * (1.0 / d)
    r_scr[...] = jnp.broadcast_to(lax.rsqrt(ms + EPS), r_scr.shape)

    def body(c, carry):
        rows = pl.ds(pl.multiple_of(c * NORM_ROWS, NORM_ROWS), NORM_ROWS)
        r = jnp.concatenate([r_scr[rows, :]] * (d // LANES), axis=1)
        h_scr[rows, :] = ((x_ref[rows, :] * r) * g_ref[...]).astype(h_scr.dtype)
        return carry

    lax.fori_loop(0, x_ref.shape[0] // NORM_ROWS, body, 0)


def _ssq_kernel(x_ref, ssq_ref):
    x = x_ref[...]
    ssq_ref[...] = _fold_lanes(x * x)


def _ssq(x):
    m, d = x.shape
    tr = _tile(m, 256, 8)
    return pl.pallas_call(
        _ssq_kernel,
        grid=(m // tr,),
        in_specs=[pl.BlockSpec((tr, d), lambda i: (i, 0))],
        out_specs=pl.BlockSpec((tr, LANES), lambda i: (i, 0)),
        out_shape=jax.ShapeDtypeStruct((m, LANES), F32),
        compiler_params=_params("parallel"),
        name="row_ssq",
    )(x)


def _mm_residual_kernel(a_ref, w_ref, r_ref, o_ref, *maybe_ssq, scale):
    acc = jnp.dot(a_ref[...], w_ref[...], preferred_element_type=F32)
    xn = r_ref[...] + scale * acc
    o_ref[...] = xn
    if maybe_ssq:
        (ssq_ref,) = maybe_ssq

        @pl.when(pl.program_id(1) == 0)
        def _():
            ssq_ref[...] = jnp.zeros(ssq_ref.shape, F32)

        ssq_ref[...] += _fold_lanes(xn * xn)


def _matmul_residual(a, w, res, scale, emit_ssq, *, tm, tn):
    m, k = a.shape
    n = w.shape[1]
    tm, tn = _tile(m, tm, 8), _tile(n, tn)
    out_specs = [pl.BlockSpec((tm, tn), lambda i, j: (i, j))]
    out_shape = [jax.ShapeDtypeStruct((m, n), F32)]
    if emit_ssq:
        out_specs.append(pl.BlockSpec((tm, LANES), lambda i, j: (i, 0)))
        out_shape.append(jax.ShapeDtypeStruct((m, LANES), F32))
    outs = pl.pallas_call(
        functools.partial(_mm_residual_kernel, scale=scale),
        grid=(m // tm, n // tn),
        in_specs=[pl.BlockSpec((tm, k), lambda i, j: (i, 0)),
                  pl.BlockSpec((k, tn), lambda i, j: (0, j)),
                  pl.BlockSpec((tm, tn), lambda i, j: (i, j))],
        out_specs=out_specs,
        out_shape=out_shape,
        compiler_params=_params("parallel", "arbitrary"),
        name="matmul_residual",
    )(a, w, res)
    return (outs[0], outs[1]) if emit_ssq else (outs[0], None)


def _ffn_up_kernel(x_ref, ssq_ref, g_ref, wa_ref, wb_ref, o_ref, h_scr, r_scr):
    @pl.when(pl.program_id(1) == 0)
    def _():
        _normalize_rows(x_ref, ssq_ref, g_ref, h_scr, r_scr)

    h = h_scr[...]
    a = jnp.dot(h, wa_ref[...], preferred_element_type=F32)
    b = jnp.dot(h, wb_ref[...], preferred_element_type=F32)
    o_ref[...] = ((a * jax.nn.sigmoid(a)) * b).astype(o_ref.dtype)


def _ffn_up(x, ssq, norm_g, w13):
    m, k = x.shape
    f = w13.shape[1] // 2
    tm, tn = _tile(m, 1024, 8), _tile(f, 256)
    nb = f // tn
    return pl.pallas_call(
        _ffn_up_kernel,
        grid=(m // tm, nb),
        in_specs=[pl.BlockSpec((tm, k), lambda i, j: (i, 0)),
                  pl.BlockSpec((tm, LANES), lambda i, j: (i, 0)),
                  pl.BlockSpec((1, k), lambda i, j: (0, 0)),
                  pl.BlockSpec((k, tn), lambda i, j: (0, j)),
                  pl.BlockSpec((k, tn), lambda i, j: (0, j + nb))],
        out_specs=pl.BlockSpec((tm, tn), lambda i, j: (i, j)),
        out_shape=jax.ShapeDtypeStruct((m, f), BF16),
        scratch_shapes=[pltpu.VMEM((tm, k), BF16), pltpu.VMEM((tm, LANES), F32)],
        compiler_params=_params("parallel", "arbitrary"),
        name="ffn_up",
    )(x, ssq, norm_g.reshape(1, k), w13, w13)


def _ffn(x, ssq, norm_g, w13, w2, emit_ssq):
    u = _ffn_up(x, ssq, norm_g, w13)
    return _matmul_residual(u, w2, x, FFN_HALF, emit_ssq, tm=512, tn=512)


def _rope_tables(seq_len, head_dim):
    axis = head_dim // 2
    pos = jnp.arange(seq_len, dtype=jnp.int32)
    row_ids = (pos // GRID_W).astype(F32)
    col_ids = (pos % GRID_W).astype(F32)
    inv_freq = ROPE_THETA ** (-jnp.arange(0, axis, 2, dtype=F32) / axis)
    ang_r = row_ids[:, None] * inv_freq[None, :]
    ang_c = col_ids[:, None] * inv_freq[None, :]
    cos = jnp.concatenate([jnp.cos(ang_r), jnp.cos(ang_r), jnp.cos(ang_c), jnp.cos(ang_c)], axis=-1)
    sin = jnp.concatenate([-jnp.sin(ang_r), jnp.sin(ang_r), -jnp.sin(ang_c), jnp.sin(ang_c)], axis=-1)
    return cos, sin


def _qkv_kernel(x_ref, ssq_ref, g_ref, w_ref, gain_ref, cos_ref, sin_ref, o_ref, h_scr, r_scr,
                *, hd, n_q_blocks, n_qk_blocks, q_scale):
    j = pl.program_id(1)

    @pl.when(j == 0)
    def _():
        _normalize_rows(x_ref, ssq_ref, g_ref, h_scr, r_scr)

    acc = jnp.dot(h_scr[...], w_ref[...], preferred_element_type=F32)
    heads = acc.shape[1] // hd
    quarter = hd // 4

    def norm_rope(post_scale):
        cos = cos_ref[...]
        sin = sin_ref[...]
        lane = lax.broadcasted_iota(jnp.int32, cos.shape, 1)
        first_half = (lane % (2 * quarter)) < quarter
        for hh in range(heads):
            x = acc[:, hh * hd:(hh + 1) * hd]
            ms = jnp.mean(x * x, axis=-1, keepdims=True)
            y = (x * lax.rsqrt(ms + EPS)) * gain_ref[:, hh * hd:(hh + 1) * hd]
            partner = jnp.where(first_half, pltpu.roll(y, hd - quarter, 1), pltpu.roll(y, quarter, 1))
            out = y * cos + partner * sin
            if post_scale != 1.0:
                out = out * post_scale
            o_ref[:, hh * hd:(hh + 1) * hd] = out.astype(o_ref.dtype)

    @pl.when(j < n_q_blocks)
    def _():
        norm_rope(q_scale)

    @pl.when(jnp.logical_and(j >= n_q_blocks, j < n_qk_blocks))
    def _():
        norm_rope(1.0)

    @pl.when(j >= n_qk_blocks)
    def _():
        o_ref[...] = acc.astype(o_ref.dtype)


def _qkv_proj(x, ssq, norm_g, w_qkv, q_norm, k_norm, cos, sin, d_model, hd):
    m, k = x.shape
    n = w_qkv.shape[1]
    kvd = (n - d_model) // 2
    tm = _tile(m, 512, 8)
    tn = _tile(kvd, 1024)
    assert d_model % tn == 0 and tn % hd == 0
    gain = jnp.concatenate([jnp.tile(q_norm, d_model // hd), jnp.tile(k_norm, kvd // hd),
                            jnp.ones((kvd,), F32)]).reshape(1, n)
    kern = functools.partial(_qkv_kernel, hd=hd, n_q_blocks=d_model // tn,
                             n_qk_blocks=(d_model + kvd) // tn, q_scale=hd ** -0.5 * LOG2_E)
    return pl.pallas_call(
        kern,
        grid=(m // tm, n // tn),
        in_specs=[pl.BlockSpec((tm, k), lambda i, j: (i, 0)),
                  pl.BlockSpec((tm, LANES), lambda i, j: (i, 0)),
                  pl.BlockSpec((1, k), lambda i, j: (0, 0)),
                  pl.BlockSpec((k, tn), lambda i, j: (0, j)),
                  pl.BlockSpec((1, tn), lambda i, j: (0, j)),
                  pl.BlockSpec((tm, hd), lambda i, j: (i, 0)),
                  pl.BlockSpec((tm, hd), lambda i, j: (i, 0))],
        out_specs=pl.BlockSpec((tm, tn), lambda i, j: (i, j)),
        out_shape=jax.ShapeDtypeStruct((m, n), BF16),
        scratch_shapes=[pltpu.VMEM((tm, k), BF16), pltpu.VMEM((tm, LANES), F32)],
        compiler_params=_params("parallel", "arbitrary"),
        name="qkv_proj",
    )(x, ssq, norm_g.reshape(1, k), w_qkv, gain, cos, sin)


def _flash_kernel(q_ref, k_ref, v_ref, o_ref, m_scr, acc_scr, a_scr, p_scr, s_a, s_b,
                  *, group, hd, tk, unroll):
    tq = q_ref.shape[0]
    nk = k_ref.shape[0] // tk
    q = jnp.concatenate([q_ref[:, g * hd:(g + 1) * hd] for g in range(group)], axis=0)
    m_scr[...] = jnp.full(m_scr.shape, -jnp.inf, F32)
    acc_scr[...] = jnp.zeros(acc_scr.shape, F32)

    bufs = (s_a, s_b)

    def scores(c, s_out):
        r0 = pl.multiple_of(c * tk, tk)
        s_out[...] = lax.dot_general(q, k_ref[pl.ds(r0, tk), :], (((1,), (1,)), ((), ())),
                                     preferred_element_type=F32)

    def update(c, s_in):
        r0 = pl.multiple_of(c * tk, tk)
        for rb in range(0, group * tq, SOFTMAX_ROWS):
            rows = slice(rb, rb + SOFTMAX_ROWS)
            s = s_in[rows, :]
            m_prev = m_scr[rows, :]
            m_new = jnp.maximum(m_prev, jnp.max(s, axis=-1, keepdims=True))
            a_scr[rows, :] = jnp.exp2(m_prev - m_new)
            m_scr[rows, :] = m_new
            p_scr[rows, :] = jnp.exp2(s - jnp.concatenate([m_new] * (tk // LANES), axis=1)).astype(BF16)
        v_ext = jnp.concatenate([v_ref[pl.ds(r0, tk), :], jnp.ones((tk, LANES), BF16)], axis=1)
        alpha = a_scr[...]
        acc_scr[...] = (jnp.concatenate([alpha, alpha], axis=1) * acc_scr[...]
                        + jnp.dot(p_scr[...], v_ext, preferred_element_type=F32))

    scores(0, s_a)

    def steady(jj, carry):
        c0 = unroll * jj
        for u in range(unroll):
            scores(c0 + u + 1, bufs[(u + 1) % 2])
            update(c0 + u, bufs[u % 2])
        return carry

    lax.fori_loop(0, nk // unroll - 1, steady, 0)
    c0 = nk - unroll
    for u in range(unroll):
        if u + 1 < unroll:
            scores(c0 + u + 1, bufs[(u + 1) % 2])
        update(c0 + u, bufs[u % 2])
    out = acc_scr[:, 0:hd] / acc_scr[:, hd:2 * hd]
    for g in range(group):
        o_ref[:, g * hd:(g + 1) * hd] = out[g * tq:(g + 1) * tq].astype(o_ref.dtype)


def _flash_attention(qkv, row0, seq, d_model, hd):
    n = qkv.shape[1]
    kvd = (n - d_model) // 2
    n_kv = kvd // hd
    group = d_model // kvd
    tq = _tile(seq, 256, 8)
    tk = _tile(seq // 2, 512)
    nk = seq // tk
    unroll = FLASH_UNROLL if nk % FLASH_UNROLL == 0 else 2
    rows = group * tq
    assert hd == LANES and row0 % seq == 0 and row0 % tq == 0 and nk % unroll == 0
    assert rows % SOFTMAX_ROWS == 0
    qb0, sb0 = row0 // tq, row0 // seq
    kc0, vc0 = d_model // hd, (d_model + kvd) // hd
    kern = functools.partial(_flash_kernel, group=group, hd=hd, tk=tk, unroll=unroll)
    return pl.pallas_call(
        kern,
        grid=(n_kv, seq // tq),
        in_specs=[pl.BlockSpec((tq, group * hd), lambda h, i: (qb0 + i, h)),
                  pl.BlockSpec((seq, hd), lambda h, i: (sb0, kc0 + h)),
                  pl.BlockSpec((seq, hd), lambda h, i: (sb0, vc0 + h))],
        out_specs=pl.BlockSpec((tq, group * hd), lambda h, i: (i, h)),
        out_shape=jax.ShapeDtypeStruct((seq, d_model), BF16),
        scratch_shapes=[pltpu.VMEM((rows, LANES), F32),
                        pltpu.VMEM((rows, 2 * hd), F32),
                        pltpu.VMEM((rows, LANES), F32),
                        pltpu.VMEM((rows, tk), BF16),
                        pltpu.VMEM((rows, tk), F32),
                        pltpu.VMEM((rows, tk), F32)],
        compiler_params=_params("parallel", "arbitrary"),
        name="flash_attention",
    )(qkv, qkv, qkv)


def _gla_in_kernel(x_ref, ssq_ref, g_ref, w_ref, wz_ref, o_ref, z_ref, h_scr, r_scr):
    @pl.when(pl.program_id(1) == 0)
    def _():
        _normalize_rows(x_ref, ssq_ref, g_ref, h_scr, r_scr)
        z_ref[...] = jnp.dot(h_scr[...], wz_ref[...], preferred_element_type=F32)

    o_ref[...] = jnp.dot(h_scr[...], w_ref[...], preferred_element_type=F32)


def _gla_in_proj(x, ssq, norm_g, w_main, w_z):
    m, k = x.shape
    n = w_main.shape[1]
    tm, tn = _tile(m, 512, 8), _tile(n, 1024)
    return pl.pallas_call(
        _gla_in_kernel,
        grid=(m // tm, n // tn),
        in_specs=[pl.BlockSpec((tm, k), lambda i, j: (i, 0)),
                  pl.BlockSpec((tm, LANES), lambda i, j: (i, 0)),
                  pl.BlockSpec((1, k), lambda i, j: (0, 0)),
                  pl.BlockSpec((k, tn), lambda i, j: (0, j)),
                  pl.BlockSpec((k, LANES), lambda i, j: (0, 0))],
        out_specs=[pl.BlockSpec((tm, tn), lambda i, j: (i, j)),
                   pl.BlockSpec((tm, LANES), lambda i, j: (i, 0))],
        out_shape=[jax.ShapeDtypeStruct((m, n), F32), jax.ShapeDtypeStruct((m, LANES), F32)],
        scratch_shapes=[pltpu.VMEM((tm, k), BF16), pltpu.VMEM((tm, LANES), F32)],
        compiler_params=_params("parallel", "arbitrary"),
        name="gla_in_proj",
    )(x, ssq, norm_g.reshape(1, k), w_main, w_z)


def _log_sigmoid(x):
    return jnp.minimum(x, 0.0) - jnp.log1p(jnp.exp(-jnp.abs(x)))


def _gla_chunk(q, k, v, z, wg, bg, state_ref, *, reverse, scale):
    c = q.shape[0]
    dk = q.shape[1]
    row = lax.broadcasted_iota(jnp.int32, (c, c), 0)
    col = lax.broadcasted_iota(jnp.int32, (c, c), 1)
    tri = (row <= col) if reverse else (row >= col)

    pre = jnp.dot(z.astype(BF16), wg, preferred_element_type=F32) + bg
    g = _log_sigmoid(pre) / GLA_GATE_NORM
    g1 = g.astype(BF16)
    r1 = g - g1.astype(F32)
    g2 = r1.astype(BF16)
    g3 = (r1 - g2.astype(F32)).astype(BF16)
    row4 = lax.broadcasted_iota(jnp.int32, (c, 4 * c), 0)
    col4 = lax.broadcasted_iota(jnp.int32, (c, 4 * c), 1)
    src = lax.rem(col4, c)
    tri4 = jnp.logical_and((row4 <= src) if reverse else (row4 >= src), col4 < 3 * c)
    b = jnp.dot(tri4.astype(BF16), jnp.concatenate([g1, g2, g3, jnp.zeros_like(g1)], axis=0),
                preferred_element_type=F32)
    b_tot = b[0:1, :] if reverse else b[c - 1:c, :]

    qs = q * scale
    q_dec = (qs * jnp.exp(b)).astype(BF16)
    k_dec = (k * jnp.exp(-b)).astype(BF16)
    k_end = (k * jnp.exp(b_tot - b)).astype(BF16)
    vb = v.astype(BF16)

    att = lax.dot_general(q_dec, k_dec, (((1,), (1,)), ((), ())), preferred_element_type=F32)
    att = jnp.where(tri, att, 0.0).astype(BF16)
    st = state_ref[...]
    o = (jnp.dot(att, vb, preferred_element_type=F32)
         + jnp.dot(q_dec, st.astype(BF16), preferred_element_type=F32))
    decay_col = jnp.exp(jnp.transpose(jnp.broadcast_to(b_tot, (LANES, dk)))[:, 0:1])
    state_ref[...] = st * decay_col + lax.dot_general(
        k_end, vb, (((0,), (0,)), ((), ())), preferred_element_type=F32)
    return o


def _gla_fwd_kernel(q_ref, k_ref, v_ref, z_ref, wg_ref, bg_ref, o_ref, state_ref, *, nchunks, hps, scale):
    dk, dv = state_ref.shape[1], state_ref.shape[2]

    @pl.when(pl.program_id(1) == 0)
    def _():
        state_ref[...] = jnp.zeros(state_ref.shape, F32)

    for ci in range(nchunks):
        rows = slice(ci * GLA_CHUNK, (ci + 1) * GLA_CHUNK)
        for hh in range(hps):
            kc, vc = slice(hh * dk, (hh + 1) * dk), slice(hh * dv, (hh + 1) * dv)
            o_ref[rows, vc] = _gla_chunk(q_ref[rows, kc], k_ref[rows, kc], v_ref[rows, vc], z_ref[rows, :],
                                         wg_ref[:, kc], bg_ref[:, kc], state_ref.at[hh],
                                         reverse=False, scale=scale)


def _gla_bwd_kernel(q_ref, k_ref, v_ref, z_ref, wg_ref, bg_ref, of_ref, r_ref, hn_ref, o_ref, state_ref,
                    *, nchunks, hps, scale):
    dk, dv = state_ref.shape[1], state_ref.shape[2]

    @pl.when(pl.program_id(1) == 0)
    def _():
        state_ref[...] = jnp.zeros(state_ref.shape, F32)

    for ci in reversed(range(nchunks)):
        rows = slice(ci * GLA_CHUNK, (ci + 1) * GLA_CHUNK)
        for hh in range(hps):
            kc, vc = slice(hh * dk, (hh + 1) * dk), slice(hh * dv, (hh + 1) * dv)
            ob = _gla_chunk(q_ref[rows, kc], k_ref[rows, kc], v_ref[rows, vc], z_ref[rows, :],
                            wg_ref[:, kc], bg_ref[:, kc], state_ref.at[hh], reverse=True, scale=scale)
            o = of_ref[rows, vc] + ob
            ms = jnp.mean(o * o, axis=-1, keepdims=True)
            on = (o * lax.rsqrt(ms + EPS)) * hn_ref[...]
            r = r_ref[rows, vc]
            o_ref[rows, vc] = (on * (r * jax.nn.sigmoid(r))).astype(o_ref.dtype)


def _gla_direction(proj, z, wg_pad, bg, row0, seq, dk_total, dv_total, heads, *, reverse,
                   o_fwd=None, head_norm=None):
    dk, dv = dk_total // heads, dv_total // heads
    hps = GLA_HEADS_PER_STEP if heads % GLA_HEADS_PER_STEP == 0 else 1
    wk, wv = hps * dk, hps * dv
    rows = _tile(seq, 256, GLA_CHUNK)
    nchunks = rows // GLA_CHUNK
    nb = seq // rows
    assert row0 % rows == 0 and (2 * dk_total) % wv == 0 and dv_total % wv == 0
    rb0 = row0 // rows
    kc0 = dk_total // wk
    vc0 = (2 * dk_total) // wv
    rc0 = (2 * dk_total + dv_total) // wv
    if reverse:
        rblk = lambda i: rb0 + nb - 1 - i
        oblk = lambda i: nb - 1 - i
    else:
        rblk = lambda i: rb0 + i
        oblk = lambda i: i
    in_specs = [pl.BlockSpec((rows, wk), lambda h, i: (rblk(i), h)),
                pl.BlockSpec((rows, wk), lambda h, i: (rblk(i), kc0 + h)),
                pl.BlockSpec((rows, wv), lambda h, i: (rblk(i), vc0 + h)),
                pl.BlockSpec((rows, LANES), lambda h, i: (rblk(i), 0)),
                pl.BlockSpec((LANES, wk), lambda h, i: (0, h)),
                pl.BlockSpec((1, wk), lambda h, i: (0, h))]
    args = [proj, proj, proj, z, wg_pad, bg.reshape(1, dk_total)]
    scale = dk ** -0.5
    if reverse:
        in_specs += [pl.BlockSpec((rows, wv), lambda h, i: (oblk(i), h)),
                     pl.BlockSpec((rows, wv), lambda h, i: (rblk(i), rc0 + h)),
                     pl.BlockSpec((1, dv), lambda h, i: (0, 0))]
        args += [o_fwd, proj, head_norm.reshape(1, dv)]
        kern = functools.partial(_gla_bwd_kernel, nchunks=nchunks, hps=hps, scale=scale)
        out_dtype = BF16
    else:
        kern = functools.partial(_gla_fwd_kernel, nchunks=nchunks, hps=hps, scale=scale)
        out_dtype = F32
    return pl.pallas_call(
        kern,
        grid=(heads // hps, nb),
        in_specs=in_specs,
        out_specs=pl.BlockSpec((rows, wv), lambda h, i: (oblk(i), h)),
        out_shape=jax.ShapeDtypeStruct((seq, dv_total), out_dtype),
        scratch_shapes=[pltpu.VMEM((hps, dk, dv), F32)],
        compiler_params=_params("parallel", "arbitrary"),
        name="gla_bwd" if reverse else "gla_fwd",
    )(*args)


def kernel(x_prompt, x_sample, ffn_norm, ffn_w13, ffn_w2, mix_norm, attn_w_qkv, attn_q_norm, attn_k_norm, attn_w_o, gla_w_in, gla_w_gate_f, gla_b_gate_f, gla_w_gate_b, gla_b_gate_b, gla_head_norm, gla_w_o, final_norm):
    d_model = x_prompt.shape[-1]
    hd = attn_q_norm.shape[-1]
    depth = ffn_norm.shape[0]
    dk_total = gla_w_gate_f.shape[-1]
    rank = gla_w_gate_f.shape[1]
    dv_total = d_model
    gla_heads = dv_total // gla_head_norm.shape[-1]
    assert x_prompt.shape[0] == 1 and x_sample.shape[0] == 1 and 2 * rank <= LANES

    named = sorted([("sample", x_sample[0]), ("prompt", x_prompt[0])], key=lambda a: -a[1].shape[0])
    seqs = [s for _, s in named]
    lens = [s.shape[0] for s in seqs]
    starts = [sum(lens[:i]) for i in range(len(lens))]
    x = jnp.concatenate(seqs, axis=0)

    tables = [_rope_tables(n, hd) for n in lens]
    cos = jnp.concatenate([t[0] for t in tables], axis=0)
    sin = jnp.concatenate([t[1] for t in tables], axis=0)

    ssq = _ssq(x)
    for i in range(depth):
        x, ssq = _ffn(x, ssq, ffn_norm[i, 0], ffn_w13[i, 0].astype(BF16), ffn_w2[i, 0].astype(BF16), True)
        j = i // 2
        if i % 2 == 0:
            qkv = _qkv_proj(x, ssq, mix_norm[i], attn_w_qkv[j].astype(BF16), attn_q_norm[j], attn_k_norm[j],
                            cos, sin, d_model, hd)
            o = jnp.concatenate([_flash_attention(qkv, r0, n, d_model, hd) for r0, n in zip(starts, lens)], axis=0)
            x, ssq = _matmul_residual(o, attn_w_o[j].astype(BF16), x, 1.0, True, tm=1024, tn=512)
        else:
            n_main = 2 * dk_total + 2 * dv_total
            w_in = gla_w_in[j]
            w_z = jnp.zeros((d_model, LANES), BF16).at[:, :2 * rank].set(w_in[:, n_main:].astype(BF16))
            proj, z = _gla_in_proj(x, ssq, mix_norm[i], w_in[:, :n_main].astype(BF16), w_z)
            wgf = jnp.zeros((LANES, dk_total), BF16).at[:rank].set(gla_w_gate_f[j].astype(BF16))
            wgb = jnp.zeros((LANES, dk_total), BF16).at[rank:2 * rank].set(gla_w_gate_b[j].astype(BF16))
            outs = []
            for r0, n in zip(starts, lens):
                o_f = _gla_direction(proj, z, wgf, gla_b_gate_f[j], r0, n, dk_total, dv_total, gla_heads,
                                     reverse=False)
                outs.append(_gla_direction(proj, z, wgb, gla_b_gate_b[j], r0, n, dk_total, dv_total, gla_heads,
                                           reverse=True, o_fwd=o_f, head_norm=gla_head_norm[j]))
            o = jnp.concatenate(outs, axis=0)
            x, ssq = _matmul_residual(o, gla_w_o[j].astype(BF16), x, 1.0, True, tm=1024, tn=512)
        x, ssq = _ffn(x, ssq, ffn_norm[i, 1], ffn_w13[i, 1].astype(BF16), ffn_w2[i, 1].astype(BF16),
                      i + 1 < depth)

    outs = {name: _rmsnorm(x, final_norm, F32, r0, n)[None] for (name, _), r0, n in zip(named, starts, lens)}
    return (outs["prompt"], outs["sample"])
```

```python
import functools

import jax
import jax.numpy as jnp
from jax import lax
from jax.experimental import pallas as pl
from jax.experimental.pallas import tpu as pltpu

GRID_W = 64
ROPE_THETA = 10000.0
GLA_CHUNK = 64
GLA_GATE_NORM = 16.0
FFN_HALF = 0.5
EPS = 1e-6
SOFTMAX_ROWS = 32
FLASH_UNROLL = 32
NORM_ROWS = 32
LOG2_E = 1.4426950408889634

LANES = 128
V7X_VMEM_LIMIT_BYTES = 56 * 1024 * 1024

F32 = jnp.float32
BF16 = jnp.bfloat16


def _tile(dim, pref, unit=LANES):
    if dim <= pref:
        return dim
    t = (pref // unit) * unit
    while t >= unit:
        if dim % t == 0:
            return t
        t -= unit
    return dim


def _params(*sem):
    return pltpu.CompilerParams(dimension_semantics=sem, vmem_limit_bytes=V7X_VMEM_LIMIT_BYTES)


def _rmsnorm_kernel(x_ref, g_ref, o_ref):
    x = x_ref[...]
    ms = jnp.mean(x * x, axis=-1, keepdims=True)
    o_ref[...] = ((x * lax.rsqrt(ms + EPS)) * g_ref[...]).astype(o_ref.dtype)


def _rmsnorm(x, g, out_dtype, row0, nrows):
    d = x.shape[1]
    tr = _tile(nrows, 256, 8)
    assert row0 % tr == 0
    rb0 = row0 // tr
    return pl.pallas_call(
        _rmsnorm_kernel,
        grid=(nrows // tr,),
        in_specs=[pl.BlockSpec((tr, d), lambda i: (rb0 + i, 0)),
                  pl.BlockSpec((1, d), lambda i: (0, 0))],
        out_specs=pl.BlockSpec((tr, d), lambda i: (i, 0)),
        out_shape=jax.ShapeDtypeStruct((nrows, d), out_dtype),
        compiler_params=_params("parallel"),
        name="rmsnorm",
    )(x, g.reshape(1, d))


def _fold_lanes(y):
    out = y[:, 0:LANES]
    for c in range(1, y.shape[1] // LANES):
        out = out + y[:, c * LANES:(c + 1) * LANES]
    return out


def _normalize_rows(xg_ref, ssq_ref, h_scr, r_scr):
    d = xg_ref.shape[1]
    ms = jnp.sum(ssq_ref[...], axis=-1, keepdims=True) * (1.0 / d)
    r_scr[...] = jnp.broadcast_to(lax.rsqrt(ms + EPS), r_scr.shape)

    def body(c, carry):
        rows = pl.ds(pl.multiple_of(c * NORM_ROWS, NORM_ROWS), NORM_ROWS)
        r = jnp.concatenate([r_scr[rows, :]] * (d // LANES), axis=1)
        h_scr[rows, :] = (xg_ref[rows, :].astype(F32) * r).astype(h_scr.dtype)
        return carry

    lax.fori_loop(0, xg_ref.shape[0] // NORM_ROWS, body, 0)


def _prep_kernel(x_ref, g_ref, xg_ref, ssq_ref):
    x = x_ref[...]
    xg_ref[...] = (x * g_ref[...]).astype(xg_ref.dtype)
    ssq_ref[...] = _fold_lanes(x * x)


def _prep(x, g):
    m, d = x.shape
    tr = _tile(m, 256, 8)
    return pl.pallas_call(
        _prep_kernel,
        grid=(m // tr,),
        in_specs=[pl.BlockSpec((tr, d), lambda i: (i, 0)),
                  pl.BlockSpec((1, d), lambda i: (0, 0))],
        out_specs=[pl.BlockSpec((tr, d), lambda i: (i, 0)),
                   pl.BlockSpec((tr, LANES), lambda i: (i, 0))],
        out_shape=[jax.ShapeDtypeStruct((m, d), BF16), jax.ShapeDtypeStruct((m, LANES), F32)],
        compiler_params=_params("parallel"),
        name="prenorm_prep",
    )(x, g.reshape(1, d))


def _weight_spec(w, idx, tn, col_block):
    k = w.shape[-2]
    return pl.BlockSpec((None,) * len(idx) + (k, tn), lambda i, j: (*idx, 0, col_block(i, j)))


def _mm_residual_kernel(a_ref, w_ref, r_ref, *rest, scale, emit_next):
    acc = jnp.dot(a_ref[...], w_ref[...], preferred_element_type=F32)
    xn = r_ref[...] + scale * acc
    if not emit_next:
        (o_ref,) = rest
        o_ref[...] = xn
        return
    gn_ref, o_ref, xg_ref, ssq_ref = rest
    o_ref[...] = xn
    xg_ref[...] = (xn * gn_ref[...]).astype(xg_ref.dtype)

    @pl.when(pl.program_id(1) == 0)
    def _():
        ssq_ref[...] = jnp.zeros(ssq_ref.shape, F32)

    ssq_ref[...] += _fold_lanes(xn * xn)


def _matmul_residual(a, w, w_idx, res, scale, g_next, *, tm, tn):
    m, k = a.shape
    n = w.shape[-1]
    tm, tn = _tile(m, tm, 8), _tile(n, tn)
    emit_next = g_next is not None
    in_specs = [pl.BlockSpec((tm, k), lambda i, j: (i, 0)),
                _weight_spec(w, w_idx, tn, lambda i, j: j),
                pl.BlockSpec((tm, tn), lambda i, j: (i, j))]
    out_specs = [pl.BlockSpec((tm, tn), lambda i, j: (i, j))]
    out_shape = [jax.ShapeDtypeStruct((m, n), F32)]
    args = [a, w, res]
    if emit_next:
        in_specs.append(pl.BlockSpec((1, tn), lambda i, j: (0, j)))
        out_specs += [pl.BlockSpec((tm, tn), lambda i, j: (i, j)),
                      pl.BlockSpec((tm, LANES), lambda i, j: (i, 0))]
        out_shape += [jax.ShapeDtypeStruct((m, n), BF16), jax.ShapeDtypeStruct((m, LANES), F32)]
        args.append(g_next.reshape(1, n))
    outs = pl.pallas_call(
        functools.partial(_mm_residual_kernel, scale=scale, emit_next=emit_next),
        grid=(m // tm, n // tn),
        in_specs=in_specs,
        out_specs=out_specs,
        out_shape=out_shape,
        compiler_params=_params("parallel", "arbitrary"),
        name="matmul_residual",
    )(*args)
    return (outs[0], (outs[1], outs[2])) if emit_next else (outs[0], None)


def _ffn_up_kernel(xg_ref, ssq_ref, wa_ref, wb_ref, o_ref, h_scr, r_scr):
    @pl.when(pl.program_id(1) == 0)
    def _():
        _normalize_rows(xg_ref, ssq_ref, h_scr, r_scr)

    h = h_scr[...]
    a = jnp.dot(h, wa_ref[...], preferred_element_type=F32)
    b = jnp.dot(h, wb_ref[...], preferred_element_type=F32)
    o_ref[...] = ((a * jax.nn.sigmoid(a)) * b).astype(o_ref.dtype)


def _ffn_up(nx, w13, w_idx):
    xg, ssq = nx
    m, k = xg.shape
    f = w13.shape[-1] // 2
    tm, tn = _tile(m, 1024, 8), _tile(f, 256)
    nb = f // tn
    return pl.pallas_call(
        _ffn_up_kernel,
        grid=(m // tm, nb),
        in_specs=[pl.BlockSpec((tm, k), lambda i, j: (i, 0)),
                  pl.BlockSpec((tm, LANES), lambda i, j: (i, 0)),
                  _weight_spec(w13, w_idx, tn, lambda i, j: j),
                  _weight_spec(w13, w_idx, tn, lambda i, j: j + nb)],
        out_specs=pl.BlockSpec((tm, tn), lambda i, j: (i, j)),
        out_shape=jax.ShapeDtypeStruct((m, f), BF16),
        scratch_shapes=[pltpu.VMEM((tm, k), BF16), pltpu.VMEM((tm, LANES), F32)],
        compiler_params=_params("parallel", "arbitrary"),
        name="ffn_up",
    )(xg, ssq, w13, w13)


def _ffn(x, nx, w13, w2, w_idx, g_next):
    u = _ffn_up(nx, w13, w_idx)
    return _matmul_residual(u, w2, w_idx, x, FFN_HALF, g_next, tm=512, tn=512)


def _rope_tables(seq_len, head_dim):
    axis = head_dim // 2
    pos = jnp.arange(seq_len, dtype=jnp.int32)
    row_ids = (pos // GRID_W).astype(F32)
    col_ids = (pos % GRID_W).astype(F32)
    inv_freq = ROPE_THETA ** (-jnp.arange(0, axis, 2, dtype=F32) / axis)
    ang_r = row_ids[:, None] * inv_freq[None, :]
    ang_c = col_ids[:, None] * inv_freq[None, :]
    cos = jnp.concatenate([jnp.cos(ang_r), jnp.cos(ang_r), jnp.cos(ang_c), jnp.cos(ang_c)], axis=-1)
    sin = jnp.concatenate([-jnp.sin(ang_r), jnp.sin(ang_r), -jnp.sin(ang_c), jnp.sin(ang_c)], axis=-1)
    return cos, sin


def _qkv_kernel(xg_ref, ssq_ref, w_ref, gain_ref, cos_ref, sin_ref, o_ref, h_scr, r_scr,
                *, hd, n_q_blocks, n_qk_blocks, q_scale):
    j = pl.program_id(1)

    @pl.when(j == 0)
    def _():
        _normalize_rows(xg_ref, ssq_ref, h_scr, r_scr)

    acc = jnp.dot(h_scr[...], w_ref[...], preferred_element_type=F32)
    heads = acc.shape[1] // hd
    quarter = hd // 4

    def norm_rope(post_scale):
        cos = cos_ref[...]
        sin = sin_ref[...]
        lane = lax.broadcasted_iota(jnp.int32, cos.shape, 1)
        first_half = (lane % (2 * quarter)) < quarter
        for hh in range(heads):
            x = acc[:, hh * hd:(hh + 1) * hd]
            ms = jnp.mean(x * x, axis=-1, keepdims=True)
            y = (x * lax.rsqrt(ms + EPS)) * gain_ref[:, hh * hd:(hh + 1) * hd]
            partner = jnp.where(first_half, pltpu.roll(y, hd - quarter, 1), pltpu.roll(y, quarter, 1))
            out = y * cos + partner * sin
            if post_scale != 1.0:
                out = out * post_scale
            o_ref[:, hh * hd:(hh + 1) * hd] = out.astype(o_ref.dtype)

    @pl.when(j < n_q_blocks)
    def _():
        norm_rope(q_scale)

    @pl.when(jnp.logical_and(j >= n_q_blocks, j < n_qk_blocks))
    def _():
        norm_rope(1.0)

    @pl.when(j >= n_qk_blocks)
    def _():
        o_ref[...] = acc.astype(o_ref.dtype)


def _qkv_proj(nx, w_qkv, w_idx, q_norm, k_norm, cos, sin, d_model, hd):
    xg, ssq = nx
    m, k = xg.shape
    n = w_qkv.shape[-1]
    kvd = (n - d_model) // 2
    tm = _tile(m, 1024, 8)
    tn = _tile(kvd, 512)
    assert d_model % tn == 0 and tn % hd == 0
    gain = jnp.concatenate([jnp.tile(q_norm, d_model // hd), jnp.tile(k_norm, kvd // hd),
                            jnp.ones((kvd,), F32)]).reshape(1, n)
    kern = functools.partial(_qkv_kernel, hd=hd, n_q_blocks=d_model // tn,
                             n_qk_blocks=(d_model + kvd) // tn, q_scale=hd ** -0.5 * LOG2_E)
    return pl.pallas_call(
        kern,
        grid=(m // tm, n // tn),
        in_specs=[pl.BlockSpec((tm, k), lambda i, j: (i, 0)),
                  pl.BlockSpec((tm, LANES), lambda i, j: (i, 0)),
                  _weight_spec(w_qkv, w_idx, tn, lambda i, j: j),
                  pl.BlockSpec((1, tn), lambda i, j: (0, j)),
                  pl.BlockSpec((tm, hd), lambda i, j: (i, 0)),
                  pl.BlockSpec((tm, hd), lambda i, j: (i, 0))],
        out_specs=pl.BlockSpec((tm, tn), lambda i, j: (i, j)),
        out_shape=jax.ShapeDtypeStruct((m, n), BF16),
        scratch_shapes=[pltpu.VMEM((tm, k), BF16), pltpu.VMEM((tm, LANES), F32)],
        compiler_params=_params("parallel", "arbitrary"),
        name="qkv_proj",
    )(xg, ssq, w_qkv, gain, cos, sin)


def _flash_kernel(q_ref, k_ref, v_ref, o_ref, m_scr, acc_scr, a_scr, p_scr, s_a, s_b,
                  *, group, hd, tk, unroll):
    tq = q_ref.shape[0]
    nk = k_ref.shape[0] // tk
    q = jnp.concatenate([q_ref[:, g * hd:(g + 1) * hd] for g in range(group)], axis=0)
    m_scr[...] = jnp.full(m_scr.shape, -jnp.inf, F32)
    acc_scr[...] = jnp.zeros(acc_scr.shape, F32)

    bufs = (s_a, s_b)

    def scores(c, s_out):
        r0 = pl.multiple_of(c * tk, tk)
        s_out[...] = lax.dot_general(q, k_ref[pl.ds(r0, tk), :], (((1,), (1,)), ((), ())),
                                     preferred_element_type=F32)

    def update(c, s_in):
        r0 = pl.multiple_of(c * tk, tk)
        for rb in range(0, group * tq, SOFTMAX_ROWS):
            rows = slice(rb, rb + SOFTMAX_ROWS)
            s = s_in[rows, :]
            m_prev = m_scr[rows, :]
            m_new = jnp.maximum(m_prev, jnp.max(s, axis=-1, keepdims=True))
            a_scr[rows, :] = jnp.exp2(m_prev - m_new)
            m_scr[rows, :] = m_new
            p_scr[rows, :] = jnp.exp2(s - jnp.concatenate([m_new] * (tk // LANES), axis=1)).astype(BF16)
        v_ext = jnp.concatenate([v_ref[pl.ds(r0, tk), :], jnp.ones((tk, LANES), BF16)], axis=1)
        alpha = a_scr[...]
        acc_scr[...] = (jnp.concatenate([alpha, alpha], axis=1) * acc_scr[...]
                        + jnp.dot(p_scr[...], v_ext, preferred_element_type=F32))

    scores(0, s_a)

    def steady(jj, carry):
        c0 = unroll * jj
        for u in range(unroll):
            scores(c0 + u + 1, bufs[(u + 1) % 2])
            update(c0 + u, bufs[u % 2])
        return carry

    lax.fori_loop(0, nk // unroll - 1, steady, 0)
    c0 = nk - unroll
    for u in range(unroll):
        if u + 1 < unroll:
            scores(c0 + u + 1, bufs[(u + 1) % 2])
        update(c0 + u, bufs[u % 2])
    out = acc_scr[:, 0:hd] / acc_scr[:, hd:2 * hd]
    for g in range(group):
        o_ref[:, g * hd:(g + 1) * hd] = out[g * tq:(g + 1) * tq].astype(o_ref.dtype)


def _flash_attention(qkv, row0, seq, d_model, hd):
    n = qkv.shape[1]
    kvd = (n - d_model) // 2
    n_kv = kvd // hd
    group = d_model // kvd
    tq = _tile(seq, 256, 8)
    tk = _tile(seq // 2, 512)
    nk = seq // tk
    unroll = max(u for u in range(2, FLASH_UNROLL + 1, 2) if nk % u == 0)
    rows = group * tq
    assert hd == LANES and row0 % seq == 0 and row0 % tq == 0 and nk % unroll == 0
    assert rows % SOFTMAX_ROWS == 0
    qb0, sb0 = row0 // tq, row0 // seq
    kc0, vc0 = d_model // hd, (d_model + kvd) // hd
    kern = functools.partial(_flash_kernel, group=group, hd=hd, tk=tk, unroll=unroll)
    return pl.pallas_call(
        kern,
        grid=(n_kv, seq // tq),
        in_specs=[pl.BlockSpec((tq, group * hd), lambda h, i: (qb0 + i, h)),
                  pl.BlockSpec((seq, hd), lambda h, i: (sb0, kc0 + h)),
                  pl.BlockSpec((seq, hd), lambda h, i: (sb0, vc0 + h))],
        out_specs=pl.BlockSpec((tq, group * hd), lambda h, i: (i, h)),
        out_shape=jax.ShapeDtypeStruct((seq, d_model), BF16),
        scratch_shapes=[pltpu.VMEM((rows, LANES), F32),
                        pltpu.VMEM((rows, 2 * hd), F32),
                        pltpu.VMEM((rows, LANES), F32),
                        pltpu.VMEM((rows, tk), BF16),
                        pltpu.VMEM((rows, tk), F32),
                        pltpu.VMEM((rows, tk), F32)],
        compiler_params=_params("parallel", "arbitrary"),
        name="flash_attention",
    )(qkv, qkv, qkv)


def _gla_in_kernel(xg_ref, ssq_ref, w_ref, wz_ref, o_ref, z_ref, h_scr, r_scr):
    @pl.when(pl.program_id(1) == 0)
    def _():
        _normalize_rows(xg_ref, ssq_ref, h_scr, r_scr)
        z_ref[...] = jnp.dot(h_scr[...], wz_ref[...], preferred_element_type=F32)

    o_ref[...] = jnp.dot(h_scr[...], w_ref[...], preferred_element_type=F32)


def _gla_in_proj(nx, w_main, w_z):
    xg, ssq = nx
    m, k = xg.shape
    n = w_main.shape[1]
    tm, tn = _tile(m, 1024, 8), _tile(n, 512)
    return pl.pallas_call(
        _gla_in_kernel,
        grid=(m // tm, n // tn),
        in_specs=[pl.BlockSpec((tm, k), lambda i, j: (i, 0)),
                  pl.BlockSpec((tm, LANES), lambda i, j: (i, 0)),
                  pl.BlockSpec((k, tn), lambda i, j: (0, j)),
                  pl.BlockSpec((k, LANES), lambda i, j: (0, 0))],
        out_specs=[pl.BlockSpec((tm, tn), lambda i, j: (i, j)),
                   pl.BlockSpec((tm, LANES), lambda i, j: (i, 0))],
        out_shape=[jax.ShapeDtypeStruct((m, n), F32), jax.ShapeDtypeStruct((m, LANES), F32)],
        scratch_shapes=[pltpu.VMEM((tm, k), BF16), pltpu.VMEM((tm, LANES), F32)],
        compiler_params=_params("parallel", "arbitrary"),
        name="gla_in_proj",
    )(xg, ssq, w_main, w_z)


def _log_sigmoid(x):
    return jnp.minimum(x, 0.0) - jnp.log1p(jnp.exp(-jnp.abs(x)))


def _gla_chunk(q, k, v, z, wg, bg, state_ref, *, reverse, scale):
    c = q.shape[0]
    dk = q.shape[1]
    row = lax.broadcasted_iota(jnp.int32, (c, c), 0)
    col = lax.broadcasted_iota(jnp.int32, (c, c), 1)
    tri = (row <= col) if reverse else (row >= col)

    pre = jnp.dot(z.astype(BF16), wg, preferred_element_type=F32) + bg
    g = _log_sigmoid(pre) / GLA_GATE_NORM
    g1 = g.astype(BF16)
    r1 = g - g1.astype(F32)
    g2 = r1.astype(BF16)
    g3 = (r1 - g2.astype(F32)).astype(BF16)
    row4 = lax.broadcasted_iota(jnp.int32, (c, 4 * c), 0)
    col4 = lax.broadcasted_iota(jnp.int32, (c, 4 * c), 1)
    src = lax.rem(col4, c)
    tri4 = jnp.logical_and((row4 <= src) if reverse else (row4 >= src), col4 < 3 * c)
    b = jnp.dot(tri4.astype(BF16), jnp.concatenate([g1, g2, g3, jnp.zeros_like(g1)], axis=0),
                preferred_element_type=F32)
    b_tot = b[0:1, :] if reverse else b[c - 1:c, :]

    qs = q * scale
    q_dec = (qs * jnp.exp(b)).astype(BF16)
    k_dec = (k * jnp.exp(-b)).astype(BF16)
    k_end = (k * jnp.exp(b_tot - b)).astype(BF16)
    vb = v.astype(BF16)

    att = lax.dot_general(q_dec, k_dec, (((1,), (1,)), ((), ())), preferred_element_type=F32)
    att = jnp.where(tri, att, 0.0).astype(BF16)
    st = state_ref[...]
    o = (jnp.dot(att, vb, preferred_element_type=F32)
         + jnp.dot(q_dec, st.astype(BF16), preferred_element_type=F32))
    decay_col = jnp.exp(jnp.transpose(jnp.broadcast_to(b_tot, (LANES, dk)))[:, 0:1])
    state_ref[...] = st * decay_col + lax.dot_general(
        k_end, vb, (((0,), (0,)), ((), ())), preferred_element_type=F32)
    return o


def _gla_fwd_kernel(q_ref, k_ref, v_ref, z_ref, wg_ref, bg_ref, o_ref, state_ref, *, nchunks, scale):
    @pl.when(pl.program_id(1) == 0)
    def _():
        state_ref[...] = jnp.zeros(state_ref.shape, F32)

    for ci in range(nchunks):
        rows = slice(ci * GLA_CHUNK, (ci + 1) * GLA_CHUNK)
        o_ref[rows, :] = _gla_chunk(q_ref[rows, :], k_ref[rows, :], v_ref[rows, :], z_ref[rows, :],
                                    wg_ref[...], bg_ref[...], state_ref, reverse=False, scale=scale)


def _gla_bwd_kernel(q_ref, k_ref, v_ref, z_ref, wg_ref, bg_ref, of_ref, r_ref, hn_ref, o_ref, state_ref,
                    *, nchunks, scale):
    @pl.when(pl.program_id(1) == 0)
    def _():
        state_ref[...] = jnp.zeros(state_ref.shape, F32)

    for ci in reversed(range(nchunks)):
        rows = slice(ci * GLA_CHUNK, (ci + 1) * GLA_CHUNK)
        ob = _gla_chunk(q_ref[rows, :], k_ref[rows, :], v_ref[rows, :], z_ref[rows, :],
                        wg_ref[...], bg_ref[...], state_ref, reverse=True, scale=scale)
        o = of_ref[rows, :] + ob
        ms = jnp.mean(o * o, axis=-1, keepdims=True)
        on = (o * lax.rsqrt(ms + EPS)) * hn_ref[...]
        r = r_ref[rows, :]
        o_ref[rows, :] = (on * (r * jax.nn.sigmoid(r))).astype(o_ref.dtype)


def _gla_direction(proj, z, wg_pad, bg, row0, seq, dk_total, dv_total, heads, *, reverse,
                   o_fwd=None, head_norm=None):
    dk, dv = dk_total // heads, dv_total // heads
    rows = _tile(seq, 256, GLA_CHUNK)
    nchunks = rows // GLA_CHUNK
    nb = seq // rows
    assert row0 % rows == 0 and (2 * dk_total) % dv == 0
    rb0 = row0 // rows
    kc0 = dk_total // dk
    vc0 = (2 * dk_total) // dv
    rc0 = (2 * dk_total + dv_total) // dv
    if reverse:
        rblk = lambda i: rb0 + nb - 1 - i
        oblk = lambda i: nb - 1 - i
    else:
        rblk = lambda i: rb0 + i
        oblk = lambda i: i
    in_specs = [pl.BlockSpec((rows, dk), lambda h, i: (rblk(i), h)),
                pl.BlockSpec((rows, dk), lambda h, i: (rblk(i), kc0 + h)),
                pl.BlockSpec((rows, dv), lambda h, i: (rblk(i), vc0 + h)),
                pl.BlockSpec((rows, LANES), lambda h, i: (rblk(i), 0)),
                pl.BlockSpec((LANES, dk), lambda h, i: (0, h)),
                pl.BlockSpec((1, dk), lambda h, i: (0, h))]
    args = [proj, proj, proj, z, wg_pad, bg.reshape(1, dk_total)]
    scale = dk ** -0.5
    if reverse:
        in_specs += [pl.BlockSpec((rows, dv), lambda h, i: (oblk(i), h)),
                     pl.BlockSpec((rows, dv), lambda h, i: (rblk(i), rc0 + h)),
                     pl.BlockSpec((1, dv), lambda h, i: (0, 0))]
        args += [o_fwd, proj, head_norm.reshape(1, dv)]
        kern = functools.partial(_gla_bwd_kernel, nchunks=nchunks, scale=scale)
        out_dtype = BF16
    else:
        kern = functools.partial(_gla_fwd_kernel, nchunks=nchunks, scale=scale)
        out_dtype = F32
    return pl.pallas_call(
        kern,
        grid=(heads, nb),
        in_specs=in_specs,
        out_specs=pl.BlockSpec((rows, dv), lambda h, i: (oblk(i), h)),
        out_shape=jax.ShapeDtypeStruct((seq, dv_total), out_dtype),
        scratch_shapes=[pltpu.VMEM((dk, dv), F32)],
        compiler_params=_params("parallel", "arbitrary"),
        name="gla_bwd" if reverse else "gla_fwd",
    )(*args)


def kernel(x_prompt, x_sample, ffn_norm, ffn_w13, ffn_w2, mix_norm, attn_w_qkv, attn_q_norm, attn_k_norm, attn_w_o, gla_w_in, gla_w_gate_f, gla_b_gate_f, gla_w_gate_b, gla_b_gate_b, gla_head_norm, gla_w_o, final_norm):
    d_model = x_prompt.shape[-1]
    hd = attn_q_norm.shape[-1]
    depth = ffn_norm.shape[0]
    dk_total = gla_w_gate_f.shape[-1]
    rank = gla_w_gate_f.shape[1]
    dv_total = d_model
    gla_heads = dv_total // gla_head_norm.shape[-1]
    assert x_prompt.shape[0] == 1 and x_sample.shape[0] == 1 and 2 * rank <= LANES

    named = sorted([("sample", x_sample[0]), ("prompt", x_prompt[0])], key=lambda a: -a[1].shape[0])
    seqs = [s for _, s in named]
    lens = [s.shape[0] for s in seqs]
    starts = [sum(lens[:i]) for i in range(len(lens))]
    x = jnp.concatenate(seqs, axis=0)

    tables = [_rope_tables(n, hd) for n in lens]
    cos = jnp.concatenate([t[0] for t in tables], axis=0)
    sin = jnp.concatenate([t[1] for t in tables], axis=0)

    w13, w2 = ffn_w13.astype(BF16), ffn_w2.astype(BF16)
    w_qkv, w_ao, w_go = attn_w_qkv.astype(BF16), attn_w_o.astype(BF16), gla_w_o.astype(BF16)
    nx = _prep(x, ffn_norm[0, 0])
    for i in range(depth):
        x, nx = _ffn(x, nx, w13, w2, (i, 0), mix_norm[i])
        j = i // 2
        if i % 2 == 0:
            qkv = _qkv_proj(nx, w_qkv, (j,), attn_q_norm[j], attn_k_norm[j], cos, sin, d_model, hd)
            o = jnp.concatenate([_flash_attention(qkv, r0, n, d_model, hd) for r0, n in zip(starts, lens)], axis=0)
            x, nx = _matmul_residual(o, w_ao, (j,), x, 1.0, ffn_norm[i, 1], tm=1024, tn=512)
        else:
            n_main = 2 * dk_total + 2 * dv_total
            w_in = gla_w_in[j]
            w_z = jnp.zeros((d_model, LANES), BF16).at[:, :2 * rank].set(w_in[:, n_main:].astype(BF16))
            proj, z = _gla_in_proj(nx, w_in[:, :n_main].astype(BF16), w_z)
            wgf = jnp.zeros((LANES, dk_total), BF16).at[:rank].set(gla_w_gate_f[j].astype(BF16))
            wgb = jnp.zeros((LANES, dk_total), BF16).at[rank:2 * rank].set(gla_w_gate_b[j].astype(BF16))
            outs = []
            for r0, n in zip(starts, lens):
                o_f = _gla_direction(proj, z, wgf, gla_b_gate_f[j], r0, n, dk_total, dv_total, gla_heads,
                                     reverse=False)
                outs.append(_gla_direction(proj, z, wgb, gla_b_gate_b[j], r0, n, dk_total, dv_total, gla_heads,
                                           reverse=True, o_fwd=o_f, head_norm=gla_head_norm[j]))
            o = jnp.concatenate(outs, axis=0)
            x, nx = _matmul_residual(o, w_go, (j,), x, 1.0, ffn_norm[i, 1], tm=1024, tn=512)
        g_after = ffn_norm[i + 1, 0] if i + 1 < depth else None
        x, nx = _ffn(x, nx, w13, w2, (i, 1), g_after)

    outs = {name: _rmsnorm(x, final_norm, F32, r0, n)[None] for (name, _), r0, n in zip(named, starts, lens)}
    return (outs["prompt"], outs["sample"])
```

```python
import functools

import jax
import jax.numpy as jnp
from jax import lax
from jax.experimental import pallas as pl
from jax.experimental.pallas import tpu as pltpu

GRID_W = 64
ROPE_THETA = 10000.0
GLA_CHUNK = 64
GLA_GATE_NORM = 16.0
FFN_HALF = 0.5
EPS = 1e-6
SOFTMAX_ROWS = 32
FLASH_UNROLL = 32
GLA_STEP_ROWS = 1024
NORM_ROWS = 32
LOG2_E = 1.4426950408889634

LANES = 128
V7X_VMEM_LIMIT_BYTES = 56 * 1024 * 1024

F32 = jnp.float32
BF16 = jnp.bfloat16


def _tile(dim, pref, unit=LANES):
    if dim <= pref:
        return dim
    t = (pref // unit) * unit
    while t >= unit:
        if dim % t == 0:
            return t
        t -= unit
    return dim


def _params(*sem):
    return pltpu.CompilerParams(dimension_semantics=sem, vmem_limit_bytes=V7X_VMEM_LIMIT_BYTES)


def _rmsnorm_kernel(x_ref, g_ref, o_ref):
    x = x_ref[...]
    ms = jnp.mean(x * x, axis=-1, keepdims=True)
    o_ref[...] = ((x * lax.rsqrt(ms + EPS)) * g_ref[...]).astype(o_ref.dtype)


def _rmsnorm(x, g, out_dtype, row0, nrows):
    d = x.shape[1]
    tr = _tile(nrows, 256, 8)
    assert row0 % tr == 0
    rb0 = row0 // tr
    return pl.pallas_call(
        _rmsnorm_kernel,
        grid=(nrows // tr,),
        in_specs=[pl.BlockSpec((tr, d), lambda i: (rb0 + i, 0)),
                  pl.BlockSpec((1, d), lambda i: (0, 0))],
        out_specs=pl.BlockSpec((tr, d), lambda i: (i, 0)),
        out_shape=jax.ShapeDtypeStruct((nrows, d), out_dtype),
        compiler_params=_params("parallel"),
        name="rmsnorm",
    )(x, g.reshape(1, d))


def _fold_lanes(y):
    out = y[:, 0:LANES]
    for c in range(1, y.shape[1] // LANES):
        out = out + y[:, c * LANES:(c + 1) * LANES]
    return out


def _normalize_rows(xg_ref, ssq_ref, h_scr, r_scr):
    d = xg_ref.shape[1]
    ms = jnp.sum(ssq_ref[...], axis=-1, keepdims=True) * (1.0 / d)
    r_scr[...] = jnp.broadcast_to(lax.rsqrt(ms + EPS), r_scr.shape)

    def body(c, carry):
        rows = pl.ds(pl.multiple_of(c * NORM_ROWS, NORM_ROWS), NORM_ROWS)
        r = jnp.concatenate([r_scr[rows, :]] * (d // LANES), axis=1)
        h_scr[rows, :] = (xg_ref[rows, :].astype(F32) * r).astype(h_scr.dtype)
        return carry

    lax.fori_loop(0, xg_ref.shape[0] // NORM_ROWS, body, 0)


def _prep_kernel(x_ref, g_ref, xg_ref, ssq_ref):
    x = x_ref[...]
    xg_ref[...] = (x * g_ref[...]).astype(xg_ref.dtype)
    ssq_ref[...] = _fold_lanes(x * x)


def _prep(x, g):
    m, d = x.shape
    tr = _tile(m, 256, 8)
    return pl.pallas_call(
        _prep_kernel,
        grid=(m // tr,),
        in_specs=[pl.BlockSpec((tr, d), lambda i: (i, 0)),
                  pl.BlockSpec((1, d), lambda i: (0, 0))],
        out_specs=[pl.BlockSpec((tr, d), lambda i: (i, 0)),
                   pl.BlockSpec((tr, LANES), lambda i: (i, 0))],
        out_shape=[jax.ShapeDtypeStruct((m, d), BF16), jax.ShapeDtypeStruct((m, LANES), F32)],
        compiler_params=_params("parallel"),
        name="prenorm_prep",
    )(x, g.reshape(1, d))


def _weight_spec(w, idx, tn, col_block):
    k = w.shape[-2]
    return pl.BlockSpec((None,) * len(idx) + (k, tn), lambda i, j: (*idx, 0, col_block(i, j)))


def _mm_residual_kernel(a_ref, w_ref, r_ref, *rest, scale, emit_next):
    acc = jnp.dot(a_ref[...], w_ref[...].astype(BF16), preferred_element_type=F32)
    xn = r_ref[...] + scale * acc
    if not emit_next:
        (o_ref,) = rest
        o_ref[...] = xn
        return
    gn_ref, o_ref, xg_ref, ssq_ref = rest
    o_ref[...] = xn
    xg_ref[...] = (xn * gn_ref[...]).astype(xg_ref.dtype)

    @pl.when(pl.program_id(1) == 0)
    def _():
        ssq_ref[...] = jnp.zeros(ssq_ref.shape, F32)

    ssq_ref[...] += _fold_lanes(xn * xn)


def _matmul_residual(a, w, w_idx, res, scale, g_next, *, tm, tn):
    m, k = a.shape
    n = w.shape[-1]
    tm, tn = _tile(m, tm, 8), _tile(n, tn)
    emit_next = g_next is not None
    in_specs = [pl.BlockSpec((tm, k), lambda i, j: (i, 0)),
                _weight_spec(w, w_idx, tn, lambda i, j: j),
                pl.BlockSpec((tm, tn), lambda i, j: (i, j))]
    out_specs = [pl.BlockSpec((tm, tn), lambda i, j: (i, j))]
    out_shape = [jax.ShapeDtypeStruct((m, n), F32)]
    args = [a, w, res]
    if emit_next:
        in_specs.append(pl.BlockSpec((1, tn), lambda i, j: (0, j)))
        out_specs += [pl.BlockSpec((tm, tn), lambda i, j: (i, j)),
                      pl.BlockSpec((tm, LANES), lambda i, j: (i, 0))]
        out_shape += [jax.ShapeDtypeStruct((m, n), BF16), jax.ShapeDtypeStruct((m, LANES), F32)]
        args.append(g_next.reshape(1, n))
    outs = pl.pallas_call(
        functools.partial(_mm_residual_kernel, scale=scale, emit_next=emit_next),
        grid=(m // tm, n // tn),
        in_specs=in_specs,
        out_specs=out_specs,
        out_shape=out_shape,
        compiler_params=_params("parallel", "arbitrary"),
        name="matmul_residual",
    )(*args)
    return (outs[0], (outs[1], outs[2])) if emit_next else (outs[0], None)


def _ffn_up_kernel(xg_ref, ssq_ref, wa_ref, wb_ref, o_ref, h_scr, r_scr):
    @pl.when(pl.program_id(1) == 0)
    def _():
        _normalize_rows(xg_ref, ssq_ref, h_scr, r_scr)

    h = h_scr[...]
    a = jnp.dot(h, wa_ref[...].astype(BF16), preferred_element_type=F32)
    b = jnp.dot(h, wb_ref[...].astype(BF16), preferred_element_type=F32)
    o_ref[...] = ((a * jax.nn.sigmoid(a)) * b).astype(o_ref.dtype)


def _ffn_up(nx, w13, w_idx):
    xg, ssq = nx
    m, k = xg.shape
    f = w13.shape[-1] // 2
    tm, tn = _tile(m, 1024, 8), _tile(f, 256)
    nb = f // tn
    return pl.pallas_call(
        _ffn_up_kernel,
        grid=(m // tm, nb),
        in_specs=[pl.BlockSpec((tm, k), lambda i, j: (i, 0)),
                  pl.BlockSpec((tm, LANES), lambda i, j: (i, 0)),
                  _weight_spec(w13, w_idx, tn, lambda i, j: j),
                  _weight_spec(w13, w_idx, tn, lambda i, j: j + nb)],
        out_specs=pl.BlockSpec((tm, tn), lambda i, j: (i, j)),
        out_shape=jax.ShapeDtypeStruct((m, f), BF16),
        scratch_shapes=[pltpu.VMEM((tm, k), BF16), pltpu.VMEM((tm, LANES), F32)],
        compiler_params=_params("parallel", "arbitrary"),
        name="ffn_up",
    )(xg, ssq, w13, w13)


def _ffn(x, nx, w13, w2, w_idx, g_next):
    u = _ffn_up(nx, w13, w_idx)
    return _matmul_residual(u, w2, w_idx, x, FFN_HALF, g_next, tm=512, tn=512)


def _rope_tables(seq_len, head_dim):
    axis = head_dim // 2
    pos = jnp.arange(seq_len, dtype=jnp.int32)
    row_ids = (pos // GRID_W).astype(F32)
    col_ids = (pos % GRID_W).astype(F32)
    inv_freq = ROPE_THETA ** (-jnp.arange(0, axis, 2, dtype=F32) / axis)
    ang_r = row_ids[:, None] * inv_freq[None, :]
    ang_c = col_ids[:, None] * inv_freq[None, :]
    cos = jnp.concatenate([jnp.cos(ang_r), jnp.cos(ang_r), jnp.cos(ang_c), jnp.cos(ang_c)], axis=-1)
    sin = jnp.concatenate([-jnp.sin(ang_r), jnp.sin(ang_r), -jnp.sin(ang_c), jnp.sin(ang_c)], axis=-1)
    return cos, sin


def _qkv_kernel(xg_ref, ssq_ref, w_ref, gain_ref, cos_ref, sin_ref, o_ref, h_scr, r_scr,
                *, hd, n_q_blocks, n_qk_blocks, q_scale):
    j = pl.program_id(1)

    @pl.when(j == 0)
    def _():
        _normalize_rows(xg_ref, ssq_ref, h_scr, r_scr)

    acc = jnp.dot(h_scr[...], w_ref[...].astype(BF16), preferred_element_type=F32)
    heads = acc.shape[1] // hd
    quarter = hd // 4

    def norm_rope(post_scale):
        cos = cos_ref[...]
        sin = sin_ref[...]
        lane = lax.broadcasted_iota(jnp.int32, cos.shape, 1)
        first_half = (lane % (2 * quarter)) < quarter
        for hh in range(heads):
            x = acc[:, hh * hd:(hh + 1) * hd]
            ms = jnp.mean(x * x, axis=-1, keepdims=True)
            y = (x * lax.rsqrt(ms + EPS)) * gain_ref[:, hh * hd:(hh + 1) * hd]
            partner = jnp.where(first_half, pltpu.roll(y, hd - quarter, 1), pltpu.roll(y, quarter, 1))
            out = y * cos + partner * sin
            if post_scale != 1.0:
                out = out * post_scale
            o_ref[:, hh * hd:(hh + 1) * hd] = out.astype(o_ref.dtype)

    @pl.when(j < n_q_blocks)
    def _():
        norm_rope(q_scale)

    @pl.when(jnp.logical_and(j >= n_q_blocks, j < n_qk_blocks))
    def _():
        norm_rope(1.0)

    @pl.when(j >= n_qk_blocks)
    def _():
        o_ref[...] = acc.astype(o_ref.dtype)


def _qkv_proj(nx, w_qkv, w_idx, q_norm, k_norm, cos, sin, d_model, hd):
    xg, ssq = nx
    m, k = xg.shape
    n = w_qkv.shape[-1]
    kvd = (n - d_model) // 2
    tm = _tile(m, 1024, 8)
    tn = _tile(kvd, 512)
    assert d_model % tn == 0 and tn % hd == 0
    gain = jnp.concatenate([jnp.tile(q_norm, d_model // hd), jnp.tile(k_norm, kvd // hd),
                            jnp.ones((kvd,), F32)]).reshape(1, n)
    kern = functools.partial(_qkv_kernel, hd=hd, n_q_blocks=d_model // tn,
                             n_qk_blocks=(d_model + kvd) // tn, q_scale=hd ** -0.5 * LOG2_E)
    return pl.pallas_call(
        kern,
        grid=(m // tm, n // tn),
        in_specs=[pl.BlockSpec((tm, k), lambda i, j: (i, 0)),
                  pl.BlockSpec((tm, LANES), lambda i, j: (i, 0)),
                  _weight_spec(w_qkv, w_idx, tn, lambda i, j: j),
                  pl.BlockSpec((1, tn), lambda i, j: (0, j)),
                  pl.BlockSpec((tm, hd), lambda i, j: (i, 0)),
                  pl.BlockSpec((tm, hd), lambda i, j: (i, 0))],
        out_specs=pl.BlockSpec((tm, tn), lambda i, j: (i, j)),
        out_shape=jax.ShapeDtypeStruct((m, n), BF16),
        scratch_shapes=[pltpu.VMEM((tm, k), BF16), pltpu.VMEM((tm, LANES), F32)],
        compiler_params=_params("parallel", "arbitrary"),
        name="qkv_proj",
    )(xg, ssq, w_qkv, gain, cos, sin)


def _flash_kernel(q_ref, k_ref, v_ref, o_ref, m_scr, acc_scr, a_scr, p_scr, s_a, s_b,
                  *, group, hd, tk, unroll):
    tq = q_ref.shape[0]
    nk = k_ref.shape[0] // tk
    q = jnp.concatenate([q_ref[:, g * hd:(g + 1) * hd] for g in range(group)], axis=0)
    m_scr[...] = jnp.full(m_scr.shape, -jnp.inf, F32)
    acc_scr[...] = jnp.zeros(acc_scr.shape, F32)

    bufs = (s_a, s_b)

    def scores(c, s_out):
        r0 = pl.multiple_of(c * tk, tk)
        s_out[...] = lax.dot_general(q, k_ref[pl.ds(r0, tk), :], (((1,), (1,)), ((), ())),
                                     preferred_element_type=F32)

    def update(c, s_in):
        r0 = pl.multiple_of(c * tk, tk)
        for rb in range(0, group * tq, SOFTMAX_ROWS):
            rows = slice(rb, rb + SOFTMAX_ROWS)
            s = s_in[rows, :]
            m_prev = m_scr[rows, :]
            m_new = jnp.maximum(m_prev, jnp.max(s, axis=-1, keepdims=True))
            a_scr[rows, :] = jnp.exp2(m_prev - m_new)
            m_scr[rows, :] = m_new
            p_scr[rows, :] = jnp.exp2(s - jnp.concatenate([m_new] * (tk // LANES), axis=1)).astype(BF16)
        v_ext = jnp.concatenate([v_ref[pl.ds(r0, tk), :], jnp.ones((tk, LANES), BF16)], axis=1)
        alpha = a_scr[...]
        acc_scr[...] = (jnp.concatenate([alpha, alpha], axis=1) * acc_scr[...]
                        + jnp.dot(p_scr[...], v_ext, preferred_element_type=F32))

    scores(0, s_a)

    def steady(jj, carry):
        c0 = unroll * jj
        for u in range(unroll):
            scores(c0 + u + 1, bufs[(u + 1) % 2])
            update(c0 + u, bufs[u % 2])
        return carry

    lax.fori_loop(0, nk // unroll - 1, steady, 0)
    c0 = nk - unroll
    for u in range(unroll):
        if u + 1 < unroll:
            scores(c0 + u + 1, bufs[(u + 1) % 2])
        update(c0 + u, bufs[u % 2])
    out = acc_scr[:, 0:hd] / acc_scr[:, hd:2 * hd]
    for g in range(group):
        o_ref[:, g * hd:(g + 1) * hd] = out[g * tq:(g + 1) * tq].astype(o_ref.dtype)


def _flash_attention(qkv, row0, seq, d_model, hd):
    n = qkv.shape[1]
    kvd = (n - d_model) // 2
    n_kv = kvd // hd
    group = d_model // kvd
    tq = _tile(seq, 256, 8)
    tk = _tile(seq // 2, 512)
    nk = seq // tk
    unroll = max(u for u in range(2, FLASH_UNROLL + 1, 2) if nk % u == 0)
    rows = group * tq
    assert hd == LANES and row0 % seq == 0 and row0 % tq == 0 and nk % unroll == 0
    assert rows % SOFTMAX_ROWS == 0
    qb0, sb0 = row0 // tq, row0 // seq
    kc0, vc0 = d_model // hd, (d_model + kvd) // hd
    kern = functools.partial(_flash_kernel, group=group, hd=hd, tk=tk, unroll=unroll)
    return pl.pallas_call(
        kern,
        grid=(n_kv, seq // tq),
        in_specs=[pl.BlockSpec((tq, group * hd), lambda h, i: (qb0 + i, h)),
                  pl.BlockSpec((seq, hd), lambda h, i: (sb0, kc0 + h)),
                  pl.BlockSpec((seq, hd), lambda h, i: (sb0, vc0 + h))],
        out_specs=pl.BlockSpec((tq, group * hd), lambda h, i: (i, h)),
        out_shape=jax.ShapeDtypeStruct((seq, d_model), BF16),
        scratch_shapes=[pltpu.VMEM((rows, LANES), F32),
                        pltpu.VMEM((rows, 2 * hd), F32),
                        pltpu.VMEM((rows, LANES), F32),
                        pltpu.VMEM((rows, tk), BF16),
                        pltpu.VMEM((rows, tk), F32),
                        pltpu.VMEM((rows, tk), F32)],
        compiler_params=_params("parallel", "arbitrary"),
        name="flash_attention",
    )(qkv, qkv, qkv)


def _gla_in_kernel(xg_ref, ssq_ref, w_ref, wz_ref, o_ref, z_ref, h_scr, r_scr):
    @pl.when(pl.program_id(1) == 0)
    def _():
        _normalize_rows(xg_ref, ssq_ref, h_scr, r_scr)
        z_ref[...] = jnp.dot(h_scr[...], wz_ref[...], preferred_element_type=F32)

    o_ref[...] = jnp.dot(h_scr[...], w_ref[...].astype(BF16), preferred_element_type=F32)


def _gla_in_proj(nx, w_in, w_idx, n, w_z):
    xg, ssq = nx
    m, k = xg.shape
    tm, tn = _tile(m, 1024, 8), _tile(n, 512)
    return pl.pallas_call(
        _gla_in_kernel,
        grid=(m // tm, n // tn),
        in_specs=[pl.BlockSpec((tm, k), lambda i, j: (i, 0)),
                  pl.BlockSpec((tm, LANES), lambda i, j: (i, 0)),
                  _weight_spec(w_in, w_idx, tn, lambda i, j: j),
                  pl.BlockSpec((k, LANES), lambda i, j: (0, 0))],
        out_specs=[pl.BlockSpec((tm, tn), lambda i, j: (i, j)),
                   pl.BlockSpec((tm, LANES), lambda i, j: (i, 0))],
        out_shape=[jax.ShapeDtypeStruct((m, n), F32), jax.ShapeDtypeStruct((m, LANES), F32)],
        scratch_shapes=[pltpu.VMEM((tm, k), BF16), pltpu.VMEM((tm, LANES), F32)],
        compiler_params=_params("parallel", "arbitrary"),
        name="gla_in_proj",
    )(xg, ssq, w_in, w_z)


def _log_sigmoid(x):
    return jnp.minimum(x, 0.0) - jnp.log1p(jnp.exp(-jnp.abs(x)))


def _gla_chunk(q, k, v, z, wg, bg, state_ref, *, reverse, scale):
    c = q.shape[0]
    dk = q.shape[1]
    row = lax.broadcasted_iota(jnp.int32, (c, c), 0)
    col = lax.broadcasted_iota(jnp.int32, (c, c), 1)
    tri = (row <= col) if reverse else (row >= col)

    pre = jnp.dot(z.astype(BF16), wg, preferred_element_type=F32) + bg
    g = _log_sigmoid(pre) / GLA_GATE_NORM
    g1 = g.astype(BF16)
    r1 = g - g1.astype(F32)
    g2 = r1.astype(BF16)
    g3 = (r1 - g2.astype(F32)).astype(BF16)
    row4 = lax.broadcasted_iota(jnp.int32, (c, 4 * c), 0)
    col4 = lax.broadcasted_iota(jnp.int32, (c, 4 * c), 1)
    src = lax.rem(col4, c)
    tri4 = jnp.logical_and((row4 <= src) if reverse else (row4 >= src), col4 < 3 * c)
    b = jnp.dot(tri4.astype(BF16), jnp.concatenate([g1, g2, g3, jnp.zeros_like(g1)], axis=0),
                preferred_element_type=F32)
    b_tot = b[0:1, :] if reverse else b[c - 1:c, :]

    qs = q * scale
    q_dec = (qs * jnp.exp(b)).astype(BF16)
    k_dec = (k * jnp.exp(-b)).astype(BF16)
    k_end = (k * jnp.exp(b_tot - b)).astype(BF16)
    vb = v.astype(BF16)

    att = lax.dot_general(q_dec, k_dec, (((1,), (1,)), ((), ())), preferred_element_type=F32)
    att = jnp.where(tri, att, 0.0).astype(BF16)
    st = state_ref[...]
    o = (jnp.dot(att, vb, preferred_element_type=F32)
         + jnp.dot(q_dec, st.astype(BF16), preferred_element_type=F32))
    decay = jnp.exp(jnp.transpose(jnp.broadcast_to(b_tot, (LANES, dk))))
    decay = jnp.concatenate([decay] * (st.shape[1] // LANES), axis=1)
    state_ref[...] = st * decay + lax.dot_general(
        k_end, vb, (((0,), (0,)), ((), ())), preferred_element_type=F32)
    return o


def _gla_fwd_kernel(q_ref, k_ref, v_ref, z_ref, wg_ref, bg_ref, o_ref, state_ref, *, nchunks, scale):
    @pl.when(pl.program_id(1) == 0)
    def _():
        state_ref[...] = jnp.zeros(state_ref.shape, F32)

    for ci in range(nchunks):
        rows = slice(ci * GLA_CHUNK, (ci + 1) * GLA_CHUNK)
        o_ref[rows, :] = _gla_chunk(q_ref[rows, :], k_ref[rows, :], v_ref[rows, :], z_ref[rows, :],
                                    wg_ref[...], bg_ref[...], state_ref, reverse=False, scale=scale)


def _gla_bwd_kernel(q_ref, k_ref, v_ref, z_ref, wg_ref, bg_ref, of_ref, r_ref, hn_ref, o_ref, state_ref,
                    *, nchunks, scale):
    @pl.when(pl.program_id(1) == 0)
    def _():
        state_ref[...] = jnp.zeros(state_ref.shape, F32)

    for ci in reversed(range(nchunks)):
        rows = slice(ci * GLA_CHUNK, (ci + 1) * GLA_CHUNK)
        ob = _gla_chunk(q_ref[rows, :], k_ref[rows, :], v_ref[rows, :], z_ref[rows, :],
                        wg_ref[...], bg_ref[...], state_ref, reverse=True, scale=scale)
        o = of_ref[rows, :] + ob
        ms = jnp.mean(o * o, axis=-1, keepdims=True)
        on = (o * lax.rsqrt(ms + EPS)) * hn_ref[...]
        r = r_ref[rows, :]
        o_ref[rows, :] = (on * (r * jax.nn.sigmoid(r))).astype(o_ref.dtype)


def _gla_direction(proj, z, wg_pad, bg, row0, seq, dk_total, dv_total, heads, *, reverse,
                   o_fwd=None, head_norm=None):
    dk, dv = dk_total // heads, dv_total // heads
    rows = _tile(seq, GLA_STEP_ROWS, GLA_CHUNK)
    nchunks = rows // GLA_CHUNK
    nb = seq // rows
    assert row0 % rows == 0 and (2 * dk_total) % dv == 0
    rb0 = row0 // rows
    kc0 = dk_total // dk
    vc0 = (2 * dk_total) // dv
    rc0 = (2 * dk_total + dv_total) // dv
    if reverse:
        rblk = lambda i: rb0 + nb - 1 - i
        oblk = lambda i: nb - 1 - i
    else:
        rblk = lambda i: rb0 + i
        oblk = lambda i: i
    in_specs = [pl.BlockSpec((rows, dk), lambda h, i: (rblk(i), h)),
                pl.BlockSpec((rows, dk), lambda h, i: (rblk(i), kc0 + h)),
                pl.BlockSpec((rows, dv), lambda h, i: (rblk(i), vc0 + h)),
                pl.BlockSpec((rows, LANES), lambda h, i: (rblk(i), 0)),
                pl.BlockSpec((LANES, dk), lambda h, i: (0, h)),
                pl.BlockSpec((1, dk), lambda h, i: (0, h))]
    args = [proj, proj, proj, z, wg_pad, bg.reshape(1, dk_total)]
    scale = dk ** -0.5
    if reverse:
        in_specs += [pl.BlockSpec((rows, dv), lambda h, i: (oblk(i), h)),
                     pl.BlockSpec((rows, dv), lambda h, i: (rblk(i), rc0 + h)),
                     pl.BlockSpec((1, dv), lambda h, i: (0, 0))]
        args += [o_fwd, proj, head_norm.reshape(1, dv)]
        kern = functools.partial(_gla_bwd_kernel, nchunks=nchunks, scale=scale)
        out_dtype = BF16
    else:
        kern = functools.partial(_gla_fwd_kernel, nchunks=nchunks, scale=scale)
        out_dtype = F32
    return pl.pallas_call(
        kern,
        grid=(heads, nb),
        in_specs=in_specs,
        out_specs=pl.BlockSpec((rows, dv), lambda h, i: (oblk(i), h)),
        out_shape=jax.ShapeDtypeStruct((seq, dv_total), out_dtype),
        scratch_shapes=[pltpu.VMEM((dk, dv), F32)],
        compiler_params=_params("parallel", "arbitrary"),
        name="gla_bwd" if reverse else "gla_fwd",
    )(*args)


def kernel(x_prompt, x_sample, ffn_norm, ffn_w13, ffn_w2, mix_norm, attn_w_qkv, attn_q_norm, attn_k_norm, attn_w_o, gla_w_in, gla_w_gate_f, gla_b_gate_f, gla_w_gate_b, gla_b_gate_b, gla_head_norm, gla_w_o, final_norm):
    d_model = x_prompt.shape[-1]
    hd = attn_q_norm.shape[-1]
    depth = ffn_norm.shape[0]
    dk_total = gla_w_gate_f.shape[-1]
    rank = gla_w_gate_f.shape[1]
    dv_total = d_model
    gla_heads = dv_total // gla_head_norm.shape[-1]
    assert x_prompt.shape[0] == 1 and x_sample.shape[0] == 1 and 2 * rank <= LANES

    named = sorted([("sample", x_sample[0]), ("prompt", x_prompt[0])], key=lambda a: -a[1].shape[0])
    seqs = [s for _, s in named]
    lens = [s.shape[0] for s in seqs]
    starts = [sum(lens[:i]) for i in range(len(lens))]
    x = jnp.concatenate(seqs, axis=0)

    tables = [_rope_tables(n, hd) for n in lens]
    cos = jnp.concatenate([t[0] for t in tables], axis=0)
    sin = jnp.concatenate([t[1] for t in tables], axis=0)

    w13, w2 = ffn_w13, ffn_w2.astype(BF16)
    w_qkv, w_ao, w_go = attn_w_qkv, attn_w_o, gla_w_o
    nx = _prep(x, ffn_norm[0, 0])
    for i in range(depth):
        x, nx = _ffn(x, nx, w13, w2, (i, 0), mix_norm[i])
        j = i // 2
        if i % 2 == 0:
            qkv = _qkv_proj(nx, w_qkv, (j,), attn_q_norm[j], attn_k_norm[j], cos, sin, d_model, hd)
            o = jnp.concatenate([_flash_attention(qkv, r0, n, d_model, hd) for r0, n in zip(starts, lens)], axis=0)
            x, nx = _matmul_residual(o, w_ao, (j,), x, 1.0, ffn_norm[i, 1], tm=1024, tn=512)
        else:
            n_main = 2 * dk_total + 2 * dv_total
            w_z = jnp.zeros((d_model, LANES), BF16).at[:, :2 * rank].set(gla_w_in[j, :, n_main:].astype(BF16))
            proj, z = _gla_in_proj(nx, gla_w_in, (j,), n_main, w_z)
            wgf = jnp.zeros((LANES, dk_total), BF16).at[:rank].set(gla_w_gate_f[j].astype(BF16))
            wgb = jnp.zeros((LANES, dk_total), BF16).at[rank:2 * rank].set(gla_w_gate_b[j].astype(BF16))
            outs = []
            for r0, n in zip(starts, lens):
                o_f = _gla_direction(proj, z, wgf, gla_b_gate_f[j], r0, n, dk_total, dv_total, gla_heads,
                                     reverse=False)
                outs.append(_gla_direction(proj, z, wgb, gla_b_gate_b[j], r0, n, dk_total, dv_total, gla_heads,
                                           reverse=True, o_fwd=o_f, head_norm=gla_head_norm[j]))
            o = jnp.concatenate(outs, axis=0)
            x, nx = _matmul_residual(o, w_go, (j,), x, 1.0, ffn_norm[i, 1], tm=1024, tn=512)
        g_after = ffn_norm[i + 1, 0] if i + 1 < depth else None
        x, nx = _ffn(x, nx, w13, w2, (i, 1), g_after)

    outs = {name: _rmsnorm(x, final_norm, F32, r0, n)[None] for (name, _), r0, n in zip(named, starts, lens)}
    return (outs["prompt"], outs["sample"])
```

```python
import collections
import functools

import jax
import jax.numpy as jnp
from jax import lax
from jax.experimental import pallas as pl
from jax.experimental.pallas import tpu as pltpu

GRID_W = 64
ROPE_THETA = 10000.0
GLA_CHUNK = 64
GLA_GATE_NORM = 16.0
FFN_HALF = 0.5
EPS = 1e-6
SOFTMAX_ROWS = 32
FLASH_UNROLL = 32
GLA_STEP_ROWS = 512
NORM_ROWS = 32
LOG2_E = 1.4426950408889634

LANES = 128
V7X_VMEM_LIMIT_BYTES = 56 * 1024 * 1024

F32 = jnp.float32
BF16 = jnp.bfloat16


def _tile(dim, pref, unit=LANES):
    if dim <= pref:
        return dim
    t = (pref // unit) * unit
    while t >= unit:
        if dim % t == 0:
            return t
        t -= unit
    return dim


def _params(*sem):
    return pltpu.CompilerParams(dimension_semantics=sem, vmem_limit_bytes=V7X_VMEM_LIMIT_BYTES)


def _rmsnorm_kernel(x_ref, g_ref, o_ref):
    x = x_ref[...]
    ms = jnp.mean(x * x, axis=-1, keepdims=True)
    o_ref[...] = ((x * lax.rsqrt(ms + EPS)) * g_ref[...]).astype(o_ref.dtype)


def _rmsnorm(x, g, out_dtype, row0, nrows):
    d = x.shape[1]
    tr = _tile(nrows, 256, 8)
    assert row0 % tr == 0
    rb0 = row0 // tr
    return pl.pallas_call(
        _rmsnorm_kernel,
        grid=(nrows // tr,),
        in_specs=[pl.BlockSpec((tr, d), lambda i: (rb0 + i, 0)),
                  pl.BlockSpec((1, d), lambda i: (0, 0))],
        out_specs=pl.BlockSpec((tr, d), lambda i: (i, 0)),
        out_shape=jax.ShapeDtypeStruct((nrows, d), out_dtype),
        compiler_params=_params("parallel"),
        name="rmsnorm",
    )(x, g.reshape(1, d))


def _fold_lanes(y):
    out = y[:, 0:LANES]
    for c in range(1, y.shape[1] // LANES):
        out = out + y[:, c * LANES:(c + 1) * LANES]
    return out


def _normalize_rows(xg_ref, ssq_ref, h_scr, r_scr):
    d = xg_ref.shape[1]
    ms = jnp.sum(ssq_ref[...], axis=-1, keepdims=True) * (1.0 / d)
    r_scr[...] = jnp.broadcast_to(lax.rsqrt(ms + EPS), r_scr.shape)

    def body(c, carry):
        rows = pl.ds(pl.multiple_of(c * NORM_ROWS, NORM_ROWS), NORM_ROWS)
        r = jnp.concatenate([r_scr[rows, :]] * (d // LANES), axis=1)
        h_scr[rows, :] = (xg_ref[rows, :].astype(F32) * r).astype(h_scr.dtype)
        return carry

    lax.fori_loop(0, xg_ref.shape[0] // NORM_ROWS, body, 0)


def _prep_kernel(x_ref, g_ref, xg_ref, ssq_ref):
    x = x_ref[...]
    xg_ref[...] = (x * g_ref[...]).astype(xg_ref.dtype)
    ssq_ref[...] = _fold_lanes(x * x)


def _prep(x, g):
    m, d = x.shape
    tr = _tile(m, 256, 8)
    return pl.pallas_call(
        _prep_kernel,
        grid=(m // tr,),
        in_specs=[pl.BlockSpec((tr, d), lambda i: (i, 0)),
                  pl.BlockSpec((1, d), lambda i: (0, 0))],
        out_specs=[pl.BlockSpec((tr, d), lambda i: (i, 0)),
                   pl.BlockSpec((tr, LANES), lambda i: (i, 0))],
        out_shape=[jax.ShapeDtypeStruct((m, d), BF16), jax.ShapeDtypeStruct((m, LANES), F32)],
        compiler_params=_params("parallel"),
        name="prenorm_prep",
    )(x, g.reshape(1, d))


def _weight_spec(w, idx, tn, col_block):
    k = w.shape[-2]
    return pl.BlockSpec((None,) * len(idx) + (k, tn), lambda i, j: (*idx, 0, col_block(i, j)))


def _mm_residual_kernel(a_ref, w_ref, r_ref, *rest, scale, emit_next):
    acc = jnp.dot(a_ref[...], w_ref[...].astype(BF16), preferred_element_type=F32)
    xn = r_ref[...] + scale * acc
    if not emit_next:
        (o_ref,) = rest
        o_ref[...] = xn
        return
    gn_ref, o_ref, xg_ref, ssq_ref = rest
    o_ref[...] = xn
    xg_ref[...] = (xn * gn_ref[...]).astype(xg_ref.dtype)

    @pl.when(pl.program_id(1) == 0)
    def _():
        ssq_ref[...] = jnp.zeros(ssq_ref.shape, F32)

    ssq_ref[...] += _fold_lanes(xn * xn)


def _matmul_residual(a, w, w_idx, res, scale, g_next, *, tm, tn):
    m, k = a.shape
    n = w.shape[-1]
    tm, tn = _tile(m, tm, 8), _tile(n, tn)
    emit_next = g_next is not None
    in_specs = [pl.BlockSpec((tm, k), lambda i, j: (i, 0)),
                _weight_spec(w, w_idx, tn, lambda i, j: j),
                pl.BlockSpec((tm, tn), lambda i, j: (i, j))]
    out_specs = [pl.BlockSpec((tm, tn), lambda i, j: (i, j))]
    out_shape = [jax.ShapeDtypeStruct((m, n), F32)]
    args = [a, w, res]
    if emit_next:
        in_specs.append(pl.BlockSpec((1, tn), lambda i, j: (0, j)))
        out_specs += [pl.BlockSpec((tm, tn), lambda i, j: (i, j)),
                      pl.BlockSpec((tm, LANES), lambda i, j: (i, 0))]
        out_shape += [jax.ShapeDtypeStruct((m, n), BF16), jax.ShapeDtypeStruct((m, LANES), F32)]
        args.append(g_next.reshape(1, n))
    outs = pl.pallas_call(
        functools.partial(_mm_residual_kernel, scale=scale, emit_next=emit_next),
        grid=(m // tm, n // tn),
        in_specs=in_specs,
        out_specs=out_specs,
        out_shape=out_shape,
        compiler_params=_params("parallel", "arbitrary"),
        name="matmul_residual",
    )(*args)
    return (outs[0], (outs[1], outs[2])) if emit_next else (outs[0], None)


def _ffn_up_kernel(xg_ref, ssq_ref, wa_ref, wb_ref, o_ref, h_scr, r_scr):
    @pl.when(pl.program_id(1) == 0)
    def _():
        _normalize_rows(xg_ref, ssq_ref, h_scr, r_scr)

    h = h_scr[...]
    a = jnp.dot(h, wa_ref[...].astype(BF16), preferred_element_type=F32)
    b = jnp.dot(h, wb_ref[...].astype(BF16), preferred_element_type=F32)
    o_ref[...] = ((a * jax.nn.sigmoid(a)) * b).astype(o_ref.dtype)


def _ffn_up(nx, w13, w_idx):
    xg, ssq = nx
    m, k = xg.shape
    f = w13.shape[-1] // 2
    tm, tn = _tile(m, 1024, 8), _tile(f, 256)
    nb = f // tn
    return pl.pallas_call(
        _ffn_up_kernel,
        grid=(m // tm, nb),
        in_specs=[pl.BlockSpec((tm, k), lambda i, j: (i, 0)),
                  pl.BlockSpec((tm, LANES), lambda i, j: (i, 0)),
                  _weight_spec(w13, w_idx, tn, lambda i, j: j),
                  _weight_spec(w13, w_idx, tn, lambda i, j: j + nb)],
        out_specs=pl.BlockSpec((tm, tn), lambda i, j: (i, j)),
        out_shape=jax.ShapeDtypeStruct((m, f), BF16),
        scratch_shapes=[pltpu.VMEM((tm, k), BF16), pltpu.VMEM((tm, LANES), F32)],
        compiler_params=_params("parallel", "arbitrary"),
        name="ffn_up",
    )(xg, ssq, w13, w13)


def _ffn(x, nx, w13, w2, w_idx, g_next):
    u = _ffn_up(nx, w13, w_idx)
    return _matmul_residual(u, w2, w_idx, x, FFN_HALF, g_next, tm=512, tn=512)


def _rope_tables(seq_len, head_dim):
    axis = head_dim // 2
    pos = jnp.arange(seq_len, dtype=jnp.int32)
    row_ids = (pos // GRID_W).astype(F32)
    col_ids = (pos % GRID_W).astype(F32)
    inv_freq = ROPE_THETA ** (-jnp.arange(0, axis, 2, dtype=F32) / axis)
    ang_r = row_ids[:, None] * inv_freq[None, :]
    ang_c = col_ids[:, None] * inv_freq[None, :]
    cos = jnp.concatenate([jnp.cos(ang_r), jnp.cos(ang_r), jnp.cos(ang_c), jnp.cos(ang_c)], axis=-1)
    sin = jnp.concatenate([-jnp.sin(ang_r), jnp.sin(ang_r), -jnp.sin(ang_c), jnp.sin(ang_c)], axis=-1)
    return cos, sin


def _qkv_kernel(xg_ref, ssq_ref, w_ref, gain_ref, cos_ref, sin_ref, o_ref, h_scr, r_scr,
                *, hd, n_q_blocks, n_qk_blocks, q_scale):
    j = pl.program_id(1)

    @pl.when(j == 0)
    def _():
        _normalize_rows(xg_ref, ssq_ref, h_scr, r_scr)

    acc = jnp.dot(h_scr[...], w_ref[...].astype(BF16), preferred_element_type=F32)
    heads = acc.shape[1] // hd
    quarter = hd // 4

    def norm_rope(post_scale):
        cos = cos_ref[...]
        sin = sin_ref[...]
        lane = lax.broadcasted_iota(jnp.int32, cos.shape, 1)
        first_half = (lane % (2 * quarter)) < quarter
        for hh in range(heads):
            x = acc[:, hh * hd:(hh + 1) * hd]
            ms = jnp.mean(x * x, axis=-1, keepdims=True)
            y = (x * lax.rsqrt(ms + EPS)) * gain_ref[:, hh * hd:(hh + 1) * hd]
            partner = jnp.where(first_half, pltpu.roll(y, hd - quarter, 1), pltpu.roll(y, quarter, 1))
            out = y * cos + partner * sin
            if post_scale != 1.0:
                out = out * post_scale
            o_ref[:, hh * hd:(hh + 1) * hd] = out.astype(o_ref.dtype)

    @pl.when(j < n_q_blocks)
    def _():
        norm_rope(q_scale)

    @pl.when(jnp.logical_and(j >= n_q_blocks, j < n_qk_blocks))
    def _():
        norm_rope(1.0)

    @pl.when(j >= n_qk_blocks)
    def _():
        o_ref[...] = acc.astype(o_ref.dtype)


def _qkv_proj(nx, w_qkv, w_idx, q_norm, k_norm, cos, sin, d_model, hd):
    xg, ssq = nx
    m, k = xg.shape
    n = w_qkv.shape[-1]
    kvd = (n - d_model) // 2
    tm = _tile(m, 1024, 8)
    tn = _tile(kvd, 512)
    assert d_model % tn == 0 and tn % hd == 0
    gain = jnp.concatenate([jnp.tile(q_norm, d_model // hd), jnp.tile(k_norm, kvd // hd),
                            jnp.ones((kvd,), F32)]).reshape(1, n)
    kern = functools.partial(_qkv_kernel, hd=hd, n_q_blocks=d_model // tn,
                             n_qk_blocks=(d_model + kvd) // tn, q_scale=hd ** -0.5 * LOG2_E)
    return pl.pallas_call(
        kern,
        grid=(m // tm, n // tn),
        in_specs=[pl.BlockSpec((tm, k), lambda i, j: (i, 0)),
                  pl.BlockSpec((tm, LANES), lambda i, j: (i, 0)),
                  _weight_spec(w_qkv, w_idx, tn, lambda i, j: j),
                  pl.BlockSpec((1, tn), lambda i, j: (0, j)),
                  pl.BlockSpec((tm, hd), lambda i, j: (i, 0)),
                  pl.BlockSpec((tm, hd), lambda i, j: (i, 0))],
        out_specs=pl.BlockSpec((tm, tn), lambda i, j: (i, j)),
        out_shape=jax.ShapeDtypeStruct((m, n), BF16),
        scratch_shapes=[pltpu.VMEM((tm, k), BF16), pltpu.VMEM((tm, LANES), F32)],
        compiler_params=_params("parallel", "arbitrary"),
        name="qkv_proj",
    )(xg, ssq, w_qkv, gain, cos, sin)


def _flash_kernel(q_ref, k_ref, v_ref, o_ref, m_scr, acc_scr, a_scr, p_scr, s_a, s_b,
                  *, group, hd, tk, unroll):
    tq = q_ref.shape[0]
    nk = k_ref.shape[0] // tk
    q = jnp.concatenate([q_ref[:, g * hd:(g + 1) * hd] for g in range(group)], axis=0)
    m_scr[...] = jnp.full(m_scr.shape, -jnp.inf, F32)
    acc_scr[...] = jnp.zeros(acc_scr.shape, F32)

    bufs = (s_a, s_b)

    def scores(c, s_out):
        r0 = pl.multiple_of(c * tk, tk)
        s_out[...] = lax.dot_general(q, k_ref[pl.ds(r0, tk), :], (((1,), (1,)), ((), ())),
                                     preferred_element_type=F32)

    def update(c, s_in):
        r0 = pl.multiple_of(c * tk, tk)
        for rb in range(0, group * tq, SOFTMAX_ROWS):
            rows = slice(rb, rb + SOFTMAX_ROWS)
            s = s_in[rows, :]
            m_prev = m_scr[rows, :]
            m_new = jnp.maximum(m_prev, jnp.max(s, axis=-1, keepdims=True))
            a_scr[rows, :] = jnp.exp2(m_prev - m_new)
            m_scr[rows, :] = m_new
            p_scr[rows, :] = jnp.exp2(s - jnp.concatenate([m_new] * (tk // LANES), axis=1)).astype(BF16)
        v_ext = jnp.concatenate([v_ref[pl.ds(r0, tk), :], jnp.ones((tk, LANES), BF16)], axis=1)
        alpha = a_scr[...]
        acc_scr[...] = (jnp.concatenate([alpha, alpha], axis=1) * acc_scr[...]
                        + jnp.dot(p_scr[...], v_ext, preferred_element_type=F32))

    scores(0, s_a)

    def steady(jj, carry):
        c0 = unroll * jj
        for u in range(unroll):
            scores(c0 + u + 1, bufs[(u + 1) % 2])
            update(c0 + u, bufs[u % 2])
        return carry

    lax.fori_loop(0, nk // unroll - 1, steady, 0)
    c0 = nk - unroll
    for u in range(unroll):
        if u + 1 < unroll:
            scores(c0 + u + 1, bufs[(u + 1) % 2])
        update(c0 + u, bufs[u % 2])
    out = acc_scr[:, 0:hd] / acc_scr[:, hd:2 * hd]
    for g in range(group):
        o_ref[:, g * hd:(g + 1) * hd] = out[g * tq:(g + 1) * tq].astype(o_ref.dtype)


def _flash_attention(qkv, row0, seq, d_model, hd):
    n = qkv.shape[1]
    kvd = (n - d_model) // 2
    n_kv = kvd // hd
    group = d_model // kvd
    tq = _tile(seq, 256, 8)
    tk = _tile(seq // 2, 512)
    nk = seq // tk
    unroll = max(u for u in range(2, FLASH_UNROLL + 1, 2) if nk % u == 0)
    rows = group * tq
    assert hd == LANES and row0 % seq == 0 and row0 % tq == 0 and nk % unroll == 0
    assert rows % SOFTMAX_ROWS == 0
    qb0, sb0 = row0 // tq, row0 // seq
    kc0, vc0 = d_model // hd, (d_model + kvd) // hd
    kern = functools.partial(_flash_kernel, group=group, hd=hd, tk=tk, unroll=unroll)
    return pl.pallas_call(
        kern,
        grid=(n_kv, seq // tq),
        in_specs=[pl.BlockSpec((tq, group * hd), lambda h, i: (qb0 + i, h)),
                  pl.BlockSpec((seq, hd), lambda h, i: (sb0, kc0 + h)),
                  pl.BlockSpec((seq, hd), lambda h, i: (sb0, vc0 + h))],
        out_specs=pl.BlockSpec((tq, group * hd), lambda h, i: (i, h)),
        out_shape=jax.ShapeDtypeStruct((seq, d_model), BF16),
        scratch_shapes=[pltpu.VMEM((rows, LANES), F32),
                        pltpu.VMEM((rows, 2 * hd), F32),
                        pltpu.VMEM((rows, LANES), F32),
                        pltpu.VMEM((rows, tk), BF16),
                        pltpu.VMEM((rows, tk), F32),
                        pltpu.VMEM((rows, tk), F32)],
        compiler_params=_params("parallel", "arbitrary"),
        name="flash_attention",
    )(qkv, qkv, qkv)


def _gla_in_kernel(xg_ref, ssq_ref, w_ref, wz_ref, o_ref, z_ref, h_scr, r_scr):
    @pl.when(pl.program_id(1) == 0)
    def _():
        _normalize_rows(xg_ref, ssq_ref, h_scr, r_scr)
        z_ref[...] = jnp.dot(h_scr[...], wz_ref[...], preferred_element_type=F32)

    o_ref[...] = jnp.dot(h_scr[...], w_ref[...].astype(BF16), preferred_element_type=F32)


def _gla_in_proj(nx, w_in, w_idx, n, w_z):
    xg, ssq = nx
    m, k = xg.shape
    tm, tn = _tile(m, 1024, 8), _tile(n, 512)
    return pl.pallas_call(
        _gla_in_kernel,
        grid=(m // tm, n // tn),
        in_specs=[pl.BlockSpec((tm, k), lambda i, j: (i, 0)),
                  pl.BlockSpec((tm, LANES), lambda i, j: (i, 0)),
                  _weight_spec(w_in, w_idx, tn, lambda i, j: j),
                  pl.BlockSpec((k, LANES), lambda i, j: (0, 0))],
        out_specs=[pl.BlockSpec((tm, tn), lambda i, j: (i, j)),
                   pl.BlockSpec((tm, LANES), lambda i, j: (i, 0))],
        out_shape=[jax.ShapeDtypeStruct((m, n), F32), jax.ShapeDtypeStruct((m, LANES), F32)],
        scratch_shapes=[pltpu.VMEM((tm, k), BF16), pltpu.VMEM((tm, LANES), F32)],
        compiler_params=_params("parallel", "arbitrary"),
        name="gla_in_proj",
    )(xg, ssq, w_in, w_z)


def _log_sigmoid(x):
    return jnp.minimum(x, 0.0) - jnp.log1p(jnp.exp(-jnp.abs(x)))


def _gla_prepare(q_ref, k_ref, z_ref, wg_ref, bg_ref, scr, *, reverse, scale):
    c = GLA_CHUNK
    dk = q_ref.shape[1]
    pre = jnp.dot(z_ref[...].astype(BF16), wg_ref[...], preferred_element_type=F32) + bg_ref[...]
    g = _log_sigmoid(pre) / GLA_GATE_NORM
    g1 = g.astype(BF16)
    r1 = g - g1.astype(F32)
    g2 = r1.astype(BF16)
    scr.gp[0] = g1
    scr.gp[1] = g2
    scr.gp[2] = (r1 - g2.astype(F32)).astype(BF16)

    row4 = lax.broadcasted_iota(jnp.int32, (c, 4 * c), 0)
    col4 = lax.broadcasted_iota(jnp.int32, (c, 4 * c), 1)
    src = lax.rem(col4, c)
    tri4 = jnp.logical_and((row4 <= src) if reverse else (row4 >= src), col4 < 3 * c).astype(BF16)
    zeros = jnp.zeros((c, dk), BF16)
    for ci in range(q_ref.shape[0] // c):
        rows = slice(ci * c, (ci + 1) * c)
        b = jnp.dot(tri4, jnp.concatenate([scr.gp[0, rows, :], scr.gp[1, rows, :], scr.gp[2, rows, :], zeros],
                                          axis=0), preferred_element_type=F32)
        b_tot = b[0:1, :] if reverse else b[c - 1:c, :]
        scr.b[rows, :] = b
        scr.bt[rows, :] = jnp.broadcast_to(b_tot, (c, dk))
        scr.dec[ci] = jnp.exp(jnp.transpose(jnp.broadcast_to(b_tot, (LANES, dk))))

    b = scr.b[...]
    k = k_ref[...]
    scr.qd[...] = ((q_ref[...] * scale) * jnp.exp(b)).astype(BF16)
    scr.kd[...] = (k * jnp.exp(-b)).astype(BF16)
    scr.ke[...] = (k * jnp.exp(scr.bt[...] - b)).astype(BF16)


def _gla_chunk(ci, v_ref, scr, state_ref, *, reverse):
    c = GLA_CHUNK
    rows = slice(ci * c, (ci + 1) * c)
    row = lax.broadcasted_iota(jnp.int32, (c, c), 0)
    col = lax.broadcasted_iota(jnp.int32, (c, c), 1)
    tri = (row <= col) if reverse else (row >= col)
    q_dec = scr.qd[rows, :]
    vb = v_ref[rows, :].astype(BF16)
    att = lax.dot_general(q_dec, scr.kd[rows, :], (((1,), (1,)), ((), ())), preferred_element_type=F32)
    att = jnp.where(tri, att, 0.0).astype(BF16)
    st = state_ref[...]
    o = (jnp.dot(att, vb, preferred_element_type=F32)
         + jnp.dot(q_dec, st.astype(BF16), preferred_element_type=F32))
    decay = jnp.concatenate([scr.dec[ci]] * (st.shape[1] // LANES), axis=1)
    state_ref[...] = st * decay + lax.dot_general(
        scr.ke[rows, :], vb, (((0,), (0,)), ((), ())), preferred_element_type=F32)
    return rows, o


_GlaScratch = collections.namedtuple("_GlaScratch", "gp b bt qd kd ke dec")


def _gla_scratch_shapes(rows, dk):
    return _GlaScratch(
        gp=pltpu.VMEM((3, rows, dk), BF16),
        b=pltpu.VMEM((rows, dk), F32),
        bt=pltpu.VMEM((rows, dk), F32),
        qd=pltpu.VMEM((rows, dk), BF16),
        kd=pltpu.VMEM((rows, dk), BF16),
        ke=pltpu.VMEM((rows, dk), BF16),
        dec=pltpu.VMEM((rows // GLA_CHUNK, dk, LANES), F32))


def _gla_fwd_kernel(q_ref, k_ref, v_ref, z_ref, wg_ref, bg_ref, o_ref, state_ref, *scratch, scale):
    scr = _GlaScratch(*scratch)

    @pl.when(pl.program_id(1) == 0)
    def _():
        state_ref[...] = jnp.zeros(state_ref.shape, F32)

    _gla_prepare(q_ref, k_ref, z_ref, wg_ref, bg_ref, scr, reverse=False, scale=scale)
    for ci in range(q_ref.shape[0] // GLA_CHUNK):
        rows, o = _gla_chunk(ci, v_ref, scr, state_ref, reverse=False)
        o_ref[rows, :] = o


def _gla_bwd_kernel(q_ref, k_ref, v_ref, z_ref, wg_ref, bg_ref, of_ref, r_ref, hn_ref, o_ref, state_ref,
                    *scratch, scale):
    scr = _GlaScratch(*scratch)

    @pl.when(pl.program_id(1) == 0)
    def _():
        state_ref[...] = jnp.zeros(state_ref.shape, F32)

    _gla_prepare(q_ref, k_ref, z_ref, wg_ref, bg_ref, scr, reverse=True, scale=scale)
    for ci in reversed(range(q_ref.shape[0] // GLA_CHUNK)):
        rows, ob = _gla_chunk(ci, v_ref, scr, state_ref, reverse=True)
        o = of_ref[rows, :] + ob
        ms = jnp.mean(o * o, axis=-1, keepdims=True)
        on = (o * lax.rsqrt(ms + EPS)) * hn_ref[...]
        r = r_ref[rows, :]
        o_ref[rows, :] = (on * (r * jax.nn.sigmoid(r))).astype(o_ref.dtype)


def _gla_direction(proj, z, wg_pad, bg, row0, seq, dk_total, dv_total, heads, *, reverse,
                   o_fwd=None, head_norm=None):
    dk, dv = dk_total // heads, dv_total // heads
    rows = _tile(seq, GLA_STEP_ROWS, GLA_CHUNK)
    nchunks = rows // GLA_CHUNK
    nb = seq // rows
    assert row0 % rows == 0 and (2 * dk_total) % dv == 0
    rb0 = row0 // rows
    kc0 = dk_total // dk
    vc0 = (2 * dk_total) // dv
    rc0 = (2 * dk_total + dv_total) // dv
    if reverse:
        rblk = lambda i: rb0 + nb - 1 - i
        oblk = lambda i: nb - 1 - i
    else:
        rblk = lambda i: rb0 + i
        oblk = lambda i: i
    in_specs = [pl.BlockSpec((rows, dk), lambda h, i: (rblk(i), h)),
                pl.BlockSpec((rows, dk), lambda h, i: (rblk(i), kc0 + h)),
                pl.BlockSpec((rows, dv), lambda h, i: (rblk(i), vc0 + h)),
                pl.BlockSpec((rows, LANES), lambda h, i: (rblk(i), 0)),
                pl.BlockSpec((LANES, dk), lambda h, i: (0, h)),
                pl.BlockSpec((1, dk), lambda h, i: (0, h))]
    args = [proj, proj, proj, z, wg_pad, bg.reshape(1, dk_total)]
    scale = dk ** -0.5
    if reverse:
        in_specs += [pl.BlockSpec((rows, dv), lambda h, i: (oblk(i), h)),
                     pl.BlockSpec((rows, dv), lambda h, i: (rblk(i), rc0 + h)),
                     pl.BlockSpec((1, dv), lambda h, i: (0, 0))]
        args += [o_fwd, proj, head_norm.reshape(1, dv)]
        kern = functools.partial(_gla_bwd_kernel, scale=scale)
        out_dtype = BF16
    else:
        kern = functools.partial(_gla_fwd_kernel, scale=scale)
        out_dtype = F32
    scratch = [pltpu.VMEM((dk, dv), F32), *_gla_scratch_shapes(rows, dk)]
    return pl.pallas_call(
        kern,
        grid=(heads, nb),
        in_specs=in_specs,
        out_specs=pl.BlockSpec((rows, dv), lambda h, i: (oblk(i), h)),
        out_shape=jax.ShapeDtypeStruct((seq, dv_total), out_dtype),
        scratch_shapes=scratch,
        compiler_params=_params("parallel", "arbitrary"),
        name="gla_bwd" if reverse else "gla_fwd",
    )(*args)


def kernel(x_prompt, x_sample, ffn_norm, ffn_w13, ffn_w2, mix_norm, attn_w_qkv, attn_q_norm, attn_k_norm, attn_w_o, gla_w_in, gla_w_gate_f, gla_b_gate_f, gla_w_gate_b, gla_b_gate_b, gla_head_norm, gla_w_o, final_norm):
    d_model = x_prompt.shape[-1]
    hd = attn_q_norm.shape[-1]
    depth = ffn_norm.shape[0]
    dk_total = gla_w_gate_f.shape[-1]
    rank = gla_w_gate_f.shape[1]
    dv_total = d_model
    gla_heads = dv_total // gla_head_norm.shape[-1]
    assert x_prompt.shape[0] == 1 and x_sample.shape[0] == 1 and 2 * rank <= LANES

    named = sorted([("sample", x_sample[0]), ("prompt", x_prompt[0])], key=lambda a: -a[1].shape[0])
    seqs = [s for _, s in named]
    lens = [s.shape[0] for s in seqs]
    starts = [sum(lens[:i]) for i in range(len(lens))]
    x = jnp.concatenate(seqs, axis=0)

    tables = [_rope_tables(n, hd) for n in lens]
    cos = jnp.concatenate([t[0] for t in tables], axis=0)
    sin = jnp.concatenate([t[1] for t in tables], axis=0)

    w13, w2 = ffn_w13, ffn_w2.astype(BF16)
    w_qkv, w_ao, w_go = attn_w_qkv, attn_w_o, gla_w_o
    nx = _prep(x, ffn_norm[0, 0])
    for i in range(depth):
        x, nx = _ffn(x, nx, w13, w2, (i, 0), mix_norm[i])
        j = i // 2
        if i % 2 == 0:
            qkv = _qkv_proj(nx, w_qkv, (j,), attn_q_norm[j], attn_k_norm[j], cos, sin, d_model, hd)
            o = jnp.concatenate([_flash_attention(qkv, r0, n, d_model, hd) for r0, n in zip(starts, lens)], axis=0)
            x, nx = _matmul_residual(o, w_ao, (j,), x, 1.0, ffn_norm[i, 1], tm=1024, tn=512)
        else:
            n_main = 2 * dk_total + 2 * dv_total
            w_z = jnp.zeros((d_model, LANES), BF16).at[:, :2 * rank].set(gla_w_in[j, :, n_main:].astype(BF16))
            proj, z = _gla_in_proj(nx, gla_w_in, (j,), n_main, w_z)
            wgf = jnp.zeros((LANES, dk_total), BF16).at[:rank].set(gla_w_gate_f[j].astype(BF16))
            wgb = jnp.zeros((LANES, dk_total), BF16).at[rank:2 * rank].set(gla_w_gate_b[j].astype(BF16))
            outs = []
            for r0, n in zip(starts, lens):
                o_f = _gla_direction(proj, z, wgf, gla_b_gate_f[j], r0, n, dk_total, dv_total, gla_heads,
                                     reverse=False)
                outs.append(_gla_direction(proj, z, wgb, gla_b_gate_b[j], r0, n, dk_total, dv_total, gla_heads,
                                           reverse=True, o_fwd=o_f, head_norm=gla_head_norm[j]))
            o = jnp.concatenate(outs, axis=0)
            x, nx = _matmul_residual(o, w_go, (j,), x, 1.0, ffn_norm[i, 1], tm=1024, tn=512)
        g_after = ffn_norm[i + 1, 0] if i + 1 < depth else None
        x, nx = _ffn(x, nx, w13, w2, (i, 1), g_after)

    outs = {name: _rmsnorm(x, final_norm, F32, r0, n)[None] for (name, _), r0, n in zip(named, starts, lens)}
    return (outs["prompt"], outs["sample"])
```

```python
import collections
import functools
import math

import jax
import jax.numpy as jnp
from jax import lax
from jax.experimental import pallas as pl
from jax.experimental.pallas import tpu as pltpu

GRID_W = 64
ROPE_THETA = 10000.0
GLA_CHUNK = 64
GLA_GATE_NORM = 16.0
FFN_HALF = 0.5
EPS = 1e-6
SOFTMAX_ROWS = 32
FLASH_UNROLL = 32
GLA_STEP_ROWS = 512
NORM_ROWS = 32
LOG2_E = 1.4426950408889634

LANES = 128
V7X_VMEM_LIMIT_BYTES = 56 * 1024 * 1024

F32 = jnp.float32
BF16 = jnp.bfloat16


def _tile(dim, pref, unit=LANES):
    if dim <= pref:
        return dim
    t = (pref // unit) * unit
    while t >= unit:
        if dim % t == 0:
            return t
        t -= unit
    return dim


def _params(*sem):
    return pltpu.CompilerParams(dimension_semantics=sem, vmem_limit_bytes=V7X_VMEM_LIMIT_BYTES)


def _rmsnorm_kernel(x_ref, g_ref, o_ref):
    x = x_ref[...]
    ms = jnp.mean(x * x, axis=-1, keepdims=True)
    o_ref[...] = ((x * lax.rsqrt(ms + EPS)) * g_ref[...]).astype(o_ref.dtype)


def _rmsnorm(x, g, out_dtype, row0, nrows):
    d = x.shape[1]
    tr = _tile(nrows, 256, 8)
    assert row0 % tr == 0
    rb0 = row0 // tr
    return pl.pallas_call(
        _rmsnorm_kernel,
        grid=(nrows // tr,),
        in_specs=[pl.BlockSpec((tr, d), lambda i: (rb0 + i, 0)),
                  pl.BlockSpec((1, d), lambda i: (0, 0))],
        out_specs=pl.BlockSpec((tr, d), lambda i: (i, 0)),
        out_shape=jax.ShapeDtypeStruct((nrows, d), out_dtype),
        compiler_params=_params("parallel"),
        name="rmsnorm",
    )(x, g.reshape(1, d))


def _fold_lanes(y):
    out = y[:, 0:LANES]
    for c in range(1, y.shape[1] // LANES):
        out = out + y[:, c * LANES:(c + 1) * LANES]
    return out


def _normalize_rows(xg_ref, ssq_ref, h_scr, r_scr):
    d = xg_ref.shape[1]
    ms = jnp.sum(ssq_ref[...], axis=-1, keepdims=True) * (1.0 / d)
    r_scr[...] = jnp.broadcast_to(lax.rsqrt(ms + EPS), r_scr.shape)

    def body(c, carry):
        rows = pl.ds(pl.multiple_of(c * NORM_ROWS, NORM_ROWS), NORM_ROWS)
        r = jnp.concatenate([r_scr[rows, :]] * (d // LANES), axis=1)
        h_scr[rows, :] = (xg_ref[rows, :].astype(F32) * r).astype(h_scr.dtype)
        return carry

    lax.fori_loop(0, xg_ref.shape[0] // NORM_ROWS, body, 0)


def _prep_kernel(xa_ref, xb_ref, g_ref, x_ref, xg_ref, ssq_ref, *, nb_a):
    def emit(src_ref):
        x = src_ref[...]
        x_ref[...] = x
        xg_ref[...] = (x * g_ref[...]).astype(xg_ref.dtype)
        ssq_ref[...] = _fold_lanes(x * x)

    @pl.when(pl.program_id(0) < nb_a)
    def _():
        emit(xa_ref)

    @pl.when(pl.program_id(0) >= nb_a)
    def _():
        emit(xb_ref)


def _prep(xa, xb, g):
    (sa, d), sb = xa.shape, xb.shape[0]
    m = sa + sb
    tr = _tile(math.gcd(sa, sb), 256, 8)
    nb_a = sa // tr
    outs = pl.pallas_call(
        functools.partial(_prep_kernel, nb_a=nb_a),
        grid=(m // tr,),
        in_specs=[pl.BlockSpec((tr, d), lambda i: (jnp.minimum(i, nb_a - 1), 0)),
                  pl.BlockSpec((tr, d), lambda i: (jnp.maximum(i - nb_a, 0), 0)),
                  pl.BlockSpec((1, d), lambda i: (0, 0))],
        out_specs=[pl.BlockSpec((tr, d), lambda i: (i, 0)),
                   pl.BlockSpec((tr, d), lambda i: (i, 0)),
                   pl.BlockSpec((tr, LANES), lambda i: (i, 0))],
        out_shape=[jax.ShapeDtypeStruct((m, d), F32), jax.ShapeDtypeStruct((m, d), BF16),
                   jax.ShapeDtypeStruct((m, LANES), F32)],
        compiler_params=_params("arbitrary"),
        name="prenorm_prep",
    )(xa, xb, g.reshape(1, d))
    return outs[0], (outs[1], outs[2])


def _weight_spec(w, idx, tn, col_block):
    k = w.shape[-2]
    return pl.BlockSpec((None,) * len(idx) + (k, tn), lambda i, j: (*idx, 0, col_block(i, j)))


def _mm_residual_kernel(a_ref, w_ref, r_ref, *rest, scale, emit_next, accumulate_ssq):
    acc = jnp.dot(a_ref[...], w_ref[...].astype(BF16), preferred_element_type=F32)
    xn = r_ref[...] + scale * acc
    if not emit_next:
        (o_ref,) = rest
        o_ref[...] = xn
        return
    gn_ref, o_ref, xg_ref, ssq_ref = rest
    o_ref[...] = xn
    xg_ref[...] = (xn * gn_ref[...]).astype(xg_ref.dtype)
    if accumulate_ssq:
        @pl.when(pl.program_id(1) == 0)
        def _():
            ssq_ref[...] = jnp.zeros(ssq_ref.shape, F32)

        ssq_ref[...] += _fold_lanes(xn * xn)
    else:
        ssq_ref[...] = _fold_lanes(xn * xn)


def _matmul_residual(a, w, w_idx, res, scale, g_next, *, tm, tn, rows_inner):
    m, k = a.shape
    n = w.shape[-1]
    tm, tn = _tile(m, tm, 8), _tile(n, tn)
    emit_next = g_next is not None

    def spec(shape, index):
        return pl.BlockSpec(shape, (lambda j, i: index(i, j)) if rows_inner else index)

    in_specs = [spec((tm, k), lambda i, j: (i, 0)),
                spec((None,) * len(w_idx) + (k, tn), lambda i, j: (*w_idx, 0, j)),
                spec((tm, tn), lambda i, j: (i, j))]
    out_specs = [spec((tm, tn), lambda i, j: (i, j))]
    out_shape = [jax.ShapeDtypeStruct((m, n), F32)]
    args = [a, w, res]
    if emit_next:
        ssq_groups = n // tn if rows_inner else 1
        in_specs.append(spec((1, tn), lambda i, j: (0, j)))
        out_specs += [spec((tm, tn), lambda i, j: (i, j)),
                      spec((tm, LANES), (lambda i, j: (i, j)) if rows_inner else (lambda i, j: (i, 0)))]
        out_shape += [jax.ShapeDtypeStruct((m, n), BF16), jax.ShapeDtypeStruct((m, ssq_groups * LANES), F32)]
        args.append(g_next.reshape(1, n))
    outs = pl.pallas_call(
        functools.partial(_mm_residual_kernel, scale=scale, emit_next=emit_next, accumulate_ssq=not rows_inner),
        grid=(n // tn, m // tm) if rows_inner else (m // tm, n // tn),
        in_specs=in_specs,
        out_specs=out_specs,
        out_shape=out_shape,
        compiler_params=_params("parallel", "arbitrary"),
        name="matmul_residual",
    )(*args)
    return (outs[0], (outs[1], outs[2])) if emit_next else (outs[0], None)


def _ffn_up_kernel(xg_ref, ssq_ref, wa_ref, wb_ref, o_ref, h_scr, r_scr):
    @pl.when(pl.program_id(1) == 0)
    def _():
        _normalize_rows(xg_ref, ssq_ref, h_scr, r_scr)

    h = h_scr[...]
    a = jnp.dot(h, wa_ref[...].astype(BF16), preferred_element_type=F32)
    b = jnp.dot(h, wb_ref[...].astype(BF16), preferred_element_type=F32)
    o_ref[...] = ((a * jax.nn.sigmoid(a)) * b).astype(o_ref.dtype)


def _ffn_up(nx, w13, w_idx):
    xg, ssq = nx
    m, k = xg.shape
    f = w13.shape[-1] // 2
    tm, tn = _tile(m, 1024, 8), _tile(f, 256)
    nb = f // tn
    return pl.pallas_call(
        _ffn_up_kernel,
        grid=(m // tm, nb),
        in_specs=[pl.BlockSpec((tm, k), lambda i, j: (i, 0)),
                  pl.BlockSpec((tm, ssq.shape[1]), lambda i, j: (i, 0)),
                  _weight_spec(w13, w_idx, tn, lambda i, j: j),
                  _weight_spec(w13, w_idx, tn, lambda i, j: j + nb)],
        out_specs=pl.BlockSpec((tm, tn), lambda i, j: (i, j)),
        out_shape=jax.ShapeDtypeStruct((m, f), BF16),
        scratch_shapes=[pltpu.VMEM((tm, k), BF16), pltpu.VMEM((tm, LANES), F32)],
        compiler_params=_params("parallel", "arbitrary"),
        name="ffn_up",
    )(xg, ssq, w13, w13)


def _ffn(x, nx, w13, w2, w_idx, g_next, rows_inner):
    u = _ffn_up(nx, w13, w_idx)
    return _matmul_residual(u, w2, w_idx, x, FFN_HALF, g_next, tm=512, tn=512, rows_inner=rows_inner)


def _rope_tables(seq_len, head_dim):
    axis = head_dim // 2
    pos = jnp.arange(seq_len, dtype=jnp.int32)
    row_ids = (pos // GRID_W).astype(F32)
    col_ids = (pos % GRID_W).astype(F32)
    inv_freq = ROPE_THETA ** (-jnp.arange(0, axis, 2, dtype=F32) / axis)
    ang_r = row_ids[:, None] * inv_freq[None, :]
    ang_c = col_ids[:, None] * inv_freq[None, :]
    cos = jnp.concatenate([jnp.cos(ang_r), jnp.cos(ang_r), jnp.cos(ang_c), jnp.cos(ang_c)], axis=-1)
    sin = jnp.concatenate([-jnp.sin(ang_r), jnp.sin(ang_r), -jnp.sin(ang_c), jnp.sin(ang_c)], axis=-1)
    return cos, sin


def _qkv_kernel(xg_ref, ssq_ref, w_ref, gain_ref, cos_ref, sin_ref, o_ref, h_scr, r_scr,
                *, hd, n_q_blocks, n_qk_blocks, q_scale):
    j = pl.program_id(1)

    @pl.when(j == 0)
    def _():
        _normalize_rows(xg_ref, ssq_ref, h_scr, r_scr)

    acc = jnp.dot(h_scr[...], w_ref[...].astype(BF16), preferred_element_type=F32)
    heads = acc.shape[1] // hd
    quarter = hd // 4

    def norm_rope(post_scale):
        cos = cos_ref[...]
        sin = sin_ref[...]
        lane = lax.broadcasted_iota(jnp.int32, cos.shape, 1)
        first_half = (lane % (2 * quarter)) < quarter
        for hh in range(heads):
            x = acc[:, hh * hd:(hh + 1) * hd]
            ms = jnp.mean(x * x, axis=-1, keepdims=True)
            y = (x * lax.rsqrt(ms + EPS)) * gain_ref[:, hh * hd:(hh + 1) * hd]
            partner = jnp.where(first_half, pltpu.roll(y, hd - quarter, 1), pltpu.roll(y, quarter, 1))
            out = y * cos + partner * sin
            if post_scale != 1.0:
                out = out * post_scale
            o_ref[:, hh * hd:(hh + 1) * hd] = out.astype(o_ref.dtype)

    @pl.when(j < n_q_blocks)
    def _():
        norm_rope(q_scale)

    @pl.when(jnp.logical_and(j >= n_q_blocks, j < n_qk_blocks))
    def _():
        norm_rope(1.0)

    @pl.when(j >= n_qk_blocks)
    def _():
        o_ref[...] = acc.astype(o_ref.dtype)


def _qkv_proj(nx, w_qkv, w_idx, q_norm, k_norm, cos, sin, d_model, hd):
    xg, ssq = nx
    m, k = xg.shape
    n = w_qkv.shape[-1]
    kvd = (n - d_model) // 2
    tm = _tile(m, 1024, 8)
    tn = _tile(kvd, 512)
    assert d_model % tn == 0 and tn % hd == 0
    gain = jnp.concatenate([jnp.tile(q_norm, d_model // hd), jnp.tile(k_norm, kvd // hd),
                            jnp.ones((kvd,), F32)]).reshape(1, n)
    kern = functools.partial(_qkv_kernel, hd=hd, n_q_blocks=d_model // tn,
                             n_qk_blocks=(d_model + kvd) // tn, q_scale=hd ** -0.5 * LOG2_E)
    return pl.pallas_call(
        kern,
        grid=(m // tm, n // tn),
        in_specs=[pl.BlockSpec((tm, k), lambda i, j: (i, 0)),
                  pl.BlockSpec((tm, ssq.shape[1]), lambda i, j: (i, 0)),
                  _weight_spec(w_qkv, w_idx, tn, lambda i, j: j),
                  pl.BlockSpec((1, tn), lambda i, j: (0, j)),
                  pl.BlockSpec((tm, hd), lambda i, j: (i, 0)),
                  pl.BlockSpec((tm, hd), lambda i, j: (i, 0))],
        out_specs=pl.BlockSpec((tm, tn), lambda i, j: (i, j)),
        out_shape=jax.ShapeDtypeStruct((m, n), BF16),
        scratch_shapes=[pltpu.VMEM((tm, k), BF16), pltpu.VMEM((tm, LANES), F32)],
        compiler_params=_params("parallel", "arbitrary"),
        name="qkv_proj",
    )(xg, ssq, w_qkv, gain, cos, sin)


def _flash_kernel(q_ref, k_ref, v_ref, o_ref, m_scr, acc_scr, a_scr, p_scr, s_a, s_b,
                  *, group, hd, tk, unroll):
    tq = q_ref.shape[0]
    nk = k_ref.shape[0] // tk
    q = jnp.concatenate([q_ref[:, g * hd:(g + 1) * hd] for g in range(group)], axis=0)
    m_scr[...] = jnp.full(m_scr.shape, -jnp.inf, F32)
    acc_scr[...] = jnp.zeros(acc_scr.shape, F32)

    bufs = (s_a, s_b)

    def scores(c, s_out):
        r0 = pl.multiple_of(c * tk, tk)
        s_out[...] = lax.dot_general(q, k_ref[pl.ds(r0, tk), :], (((1,), (1,)), ((), ())),
                                     preferred_element_type=F32)

    def update(c, s_in):
        r0 = pl.multiple_of(c * tk, tk)
        for rb in range(0, group * tq, SOFTMAX_ROWS):
            rows = slice(rb, rb + SOFTMAX_ROWS)
            s = s_in[rows, :]
            m_prev = m_scr[rows, :]
            m_new = jnp.maximum(m_prev, jnp.max(s, axis=-1, keepdims=True))
            a_scr[rows, :] = jnp.exp2(m_prev - m_new)
            m_scr[rows, :] = m_new
            p_scr[rows, :] = jnp.exp2(s - jnp.concatenate([m_new] * (tk // LANES), axis=1)).astype(BF16)
        v_ext = jnp.concatenate([v_ref[pl.ds(r0, tk), :], jnp.ones((tk, LANES), BF16)], axis=1)
        alpha = a_scr[...]
        acc_scr[...] = (jnp.concatenate([alpha, alpha], axis=1) * acc_scr[...]
                        + jnp.dot(p_scr[...], v_ext, preferred_element_type=F32))

    scores(0, s_a)

    def steady(jj, carry):
        c0 = unroll * jj
        for u in range(unroll):
            scores(c0 + u + 1, bufs[(u + 1) % 2])
            update(c0 + u, bufs[u % 2])
        return carry

    lax.fori_loop(0, nk // unroll - 1, steady, 0)
    c0 = nk - unroll
    for u in range(unroll):
        if u + 1 < unroll:
            scores(c0 + u + 1, bufs[(u + 1) % 2])
        update(c0 + u, bufs[u % 2])
    out = acc_scr[:, 0:hd] / acc_scr[:, hd:2 * hd]
    for g in range(group):
        o_ref[:, g * hd:(g + 1) * hd] = out[g * tq:(g + 1) * tq].astype(o_ref.dtype)


def _flash_attention(qkv, row0, seq, d_model, hd):
    n = qkv.shape[1]
    kvd = (n - d_model) // 2
    n_kv = kvd // hd
    group = d_model // kvd
    tq = _tile(seq, 256, 8)
    tk = _tile(seq // 2, 512)
    nk = seq // tk
    unroll = max(u for u in range(2, FLASH_UNROLL + 1, 2) if nk % u == 0)
    rows = group * tq
    assert hd == LANES and row0 % seq == 0 and row0 % tq == 0 and nk % unroll == 0
    assert rows % SOFTMAX_ROWS == 0
    qb0, sb0 = row0 // tq, row0 // seq
    kc0, vc0 = d_model // hd, (d_model + kvd) // hd
    kern = functools.partial(_flash_kernel, group=group, hd=hd, tk=tk, unroll=unroll)
    return pl.pallas_call(
        kern,
        grid=(n_kv, seq // tq),
        in_specs=[pl.BlockSpec((tq, group * hd), lambda h, i: (qb0 + i, h)),
                  pl.BlockSpec((seq, hd), lambda h, i: (sb0, kc0 + h)),
                  pl.BlockSpec((seq, hd), lambda h, i: (sb0, vc0 + h))],
        out_specs=pl.BlockSpec((tq, group * hd), lambda h, i: (i, h)),
        out_shape=jax.ShapeDtypeStruct((seq, d_model), BF16),
        scratch_shapes=[pltpu.VMEM((rows, LANES), F32),
                        pltpu.VMEM((rows, 2 * hd), F32),
                        pltpu.VMEM((rows, LANES), F32),
                        pltpu.VMEM((rows, tk), BF16),
                        pltpu.VMEM((rows, tk), F32),
                        pltpu.VMEM((rows, tk), F32)],
        compiler_params=_params("parallel", "arbitrary"),
        name="flash_attention",
    )(qkv, qkv, qkv)


def _gla_in_kernel(xg_ref, ssq_ref, w_ref, wz_ref, o_ref, z_ref, h_scr, r_scr):
    @pl.when(pl.program_id(1) == 0)
    def _():
        _normalize_rows(xg_ref, ssq_ref, h_scr, r_scr)
        z_ref[...] = jnp.dot(h_scr[...], wz_ref[...], preferred_element_type=F32)

    o_ref[...] = jnp.dot(h_scr[...], w_ref[...].astype(BF16), preferred_element_type=F32)


def _gla_in_proj(nx, w_in, w_idx, n, w_z):
    xg, ssq = nx
    m, k = xg.shape
    tm, tn = _tile(m, 1024, 8), _tile(n, 512)
    return pl.pallas_call(
        _gla_in_kernel,
        grid=(m // tm, n // tn),
        in_specs=[pl.BlockSpec((tm, k), lambda i, j: (i, 0)),
                  pl.BlockSpec((tm, ssq.shape[1]), lambda i, j: (i, 0)),
                  _weight_spec(w_in, w_idx, tn, lambda i, j: j),
                  pl.BlockSpec((k, LANES), lambda i, j: (0, 0))],
        out_specs=[pl.BlockSpec((tm, tn), lambda i, j: (i, j)),
                   pl.BlockSpec((tm, LANES), lambda i, j: (i, 0))],
        out_shape=[jax.ShapeDtypeStruct((m, n), F32), jax.ShapeDtypeStruct((m, LANES), F32)],
        scratch_shapes=[pltpu.VMEM((tm, k), BF16), pltpu.VMEM((tm, LANES), F32)],
        compiler_params=_params("parallel", "arbitrary"),
        name="gla_in_proj",
    )(xg, ssq, w_in, w_z)


def _log_sigmoid(x):
    return jnp.minimum(x, 0.0) - jnp.log1p(jnp.exp(-jnp.abs(x)))


def _gla_prepare(q_ref, k_ref, v_ref, z_ref, wg_ref, bg_ref, oi_ref, scr, *, reverse, scale):
    c = GLA_CHUNK
    dk = q_ref.shape[1]
    pre = jnp.dot(z_ref[...].astype(BF16), wg_ref[...], preferred_element_type=F32) + bg_ref[...]
    g = _log_sigmoid(pre) / GLA_GATE_NORM
    g1 = g.astype(BF16)
    r1 = g - g1.astype(F32)
    g2 = r1.astype(BF16)
    scr.gp[0] = g1
    scr.gp[1] = g2
    scr.gp[2] = (r1 - g2.astype(F32)).astype(BF16)

    row4 = lax.broadcasted_iota(jnp.int32, (c, 4 * c), 0)
    col4 = lax.broadcasted_iota(jnp.int32, (c, 4 * c), 1)
    src = lax.rem(col4, c)
    tri4 = jnp.logical_and((row4 <= src) if reverse else (row4 >= src), col4 < 3 * c).astype(BF16)
    zeros = jnp.zeros((c, dk), BF16)
    for ci in range(q_ref.shape[0] // c):
        rows = slice(ci * c, (ci + 1) * c)
        b = jnp.dot(tri4, jnp.concatenate([scr.gp[0, rows, :], scr.gp[1, rows, :], scr.gp[2, rows, :], zeros],
                                          axis=0), preferred_element_type=F32)
        b_tot = b[0:1, :] if reverse else b[c - 1:c, :]
        scr.b[rows, :] = b
        scr.bt[rows, :] = jnp.broadcast_to(b_tot, (c, dk))
        scr.dec[ci] = jnp.exp(jnp.transpose(jnp.broadcast_to(b_tot, (LANES, dk))))

    b = scr.b[...]
    k = k_ref[...]
    scr.qd[...] = ((q_ref[...] * scale) * jnp.exp(b)).astype(BF16)
    scr.kd[...] = (k * jnp.exp(-b)).astype(BF16)
    scr.ke[...] = (k * jnp.exp(scr.bt[...] - b)).astype(BF16)

    row = lax.broadcasted_iota(jnp.int32, (c, c), 0)
    col = lax.broadcasted_iota(jnp.int32, (c, c), 1)
    tri = (row <= col) if reverse else (row >= col)
    for ci in range(q_ref.shape[0] // c):
        rows = slice(ci * c, (ci + 1) * c)
        att = lax.dot_general(scr.qd[rows, :], scr.kd[rows, :], (((1,), (1,)), ((), ())),
                              preferred_element_type=F32)
        scr.att[rows, :] = jnp.where(tri, att, 0.0).astype(BF16)
    for ci in range(q_ref.shape[0] // c):
        rows = slice(ci * c, (ci + 1) * c)
        oi_ref[rows, :] = jnp.dot(scr.att[rows, :], v_ref[rows, :].astype(BF16), preferred_element_type=F32)


def _gla_chunk(ci, v_ref, scr, state_ref):
    rows = slice(ci * GLA_CHUNK, (ci + 1) * GLA_CHUNK)
    st = state_ref[...]
    o = jnp.dot(scr.qd[rows, :], st.astype(BF16), preferred_element_type=F32)
    decay = jnp.concatenate([scr.dec[ci]] * (st.shape[1] // LANES), axis=1)
    state_ref[...] = st * decay + lax.dot_general(
        scr.ke[rows, :], v_ref[rows, :].astype(BF16), (((0,), (0,)), ((), ())), preferred_element_type=F32)
    return rows, o


_GlaScratch = collections.namedtuple("_GlaScratch", "gp b bt qd kd ke dec att")


def _gla_scratch_shapes(rows, dk):
    return _GlaScratch(
        gp=pltpu.VMEM((3, rows, dk), BF16),
        b=pltpu.VMEM((rows, dk), F32),
        bt=pltpu.VMEM((rows, dk), F32),
        qd=pltpu.VMEM((rows, dk), BF16),
        kd=pltpu.VMEM((rows, dk), BF16),
        ke=pltpu.VMEM((rows, dk), BF16),
        dec=pltpu.VMEM((rows // GLA_CHUNK, dk, LANES), F32),
        att=pltpu.VMEM((rows, GLA_CHUNK), BF16))


def _gla_fwd_kernel(q_ref, k_ref, v_ref, z_ref, wg_ref, bg_ref, o_ref, state_ref, *scratch, scale):
    scr = _GlaScratch(*scratch)

    @pl.when(pl.program_id(1) == 0)
    def _():
        state_ref[...] = jnp.zeros(state_ref.shape, F32)

    _gla_prepare(q_ref, k_ref, v_ref, z_ref, wg_ref, bg_ref, o_ref, scr, reverse=False, scale=scale)
    for ci in range(q_ref.shape[0] // GLA_CHUNK):
        rows, o_inter = _gla_chunk(ci, v_ref, scr, state_ref)
        o_ref[rows, :] += o_inter


def _gla_bwd_kernel(q_ref, k_ref, v_ref, z_ref, wg_ref, bg_ref, of_ref, r_ref, hn_ref, o_ref, state_ref,
                    oi_scr, *scratch, scale):
    scr = _GlaScratch(*scratch)

    @pl.when(pl.program_id(1) == 0)
    def _():
        state_ref[...] = jnp.zeros(state_ref.shape, F32)

    _gla_prepare(q_ref, k_ref, v_ref, z_ref, wg_ref, bg_ref, oi_scr, scr, reverse=True, scale=scale)
    for ci in reversed(range(q_ref.shape[0] // GLA_CHUNK)):
        rows, o_inter = _gla_chunk(ci, v_ref, scr, state_ref)
        o = of_ref[rows, :] + (oi_scr[rows, :] + o_inter)
        ms = jnp.mean(o * o, axis=-1, keepdims=True)
        on = (o * lax.rsqrt(ms + EPS)) * hn_ref[...]
        r = r_ref[rows, :]
        o_ref[rows, :] = (on * (r * jax.nn.sigmoid(r))).astype(o_ref.dtype)


def _gla_direction(proj, z, wg_pad, bg, row0, seq, dk_total, dv_total, heads, *, reverse,
                   o_fwd=None, head_norm=None):
    dk, dv = dk_total // heads, dv_total // heads
    rows = _tile(seq, GLA_STEP_ROWS, GLA_CHUNK)
    nchunks = rows // GLA_CHUNK
    nb = seq // rows
    assert row0 % rows == 0 and (2 * dk_total) % dv == 0
    rb0 = row0 // rows
    kc0 = dk_total // dk
    vc0 = (2 * dk_total) // dv
    rc0 = (2 * dk_total + dv_total) // dv
    if reverse:
        rblk = lambda i: rb0 + nb - 1 - i
        oblk = lambda i: nb - 1 - i
    else:
        rblk = lambda i: rb0 + i
        oblk = lambda i: i
    in_specs = [pl.BlockSpec((rows, dk), lambda h, i: (rblk(i), h)),
                pl.BlockSpec((rows, dk), lambda h, i: (rblk(i), kc0 + h)),
                pl.BlockSpec((rows, dv), lambda h, i: (rblk(i), vc0 + h)),
                pl.BlockSpec((rows, LANES), lambda h, i: (rblk(i), 0)),
                pl.BlockSpec((LANES, dk), lambda h, i: (0, h)),
                pl.BlockSpec((1, dk), lambda h, i: (0, h))]
    args = [proj, proj, proj, z, wg_pad, bg.reshape(1, dk_total)]
    scale = dk ** -0.5
    if reverse:
        in_specs += [pl.BlockSpec((rows, dv), lambda h, i: (oblk(i), h)),
                     pl.BlockSpec((rows, dv), lambda h, i: (rblk(i), rc0 + h)),
                     pl.BlockSpec((1, dv), lambda h, i: (0, 0))]
        args += [o_fwd, proj, head_norm.reshape(1, dv)]
        kern = functools.partial(_gla_bwd_kernel, scale=scale)
        out_dtype = BF16
    else:
        kern = functools.partial(_gla_fwd_kernel, scale=scale)
        out_dtype = F32
    scratch = [pltpu.VMEM((dk, dv), F32)]
    if reverse:
        scratch.append(pltpu.VMEM((rows, dv), F32))
    scratch += list(_gla_scratch_shapes(rows, dk))
    return pl.pallas_call(
        kern,
        grid=(heads, nb),
        in_specs=in_specs,
        out_specs=pl.BlockSpec((rows, dv), lambda h, i: (oblk(i), h)),
        out_shape=jax.ShapeDtypeStruct((seq, dv_total), out_dtype),
        scratch_shapes=scratch,
        compiler_params=_params("parallel", "arbitrary"),
        name="gla_bwd" if reverse else "gla_fwd",
    )(*args)


def kernel(x_prompt, x_sample, ffn_norm, ffn_w13, ffn_w2, mix_norm, attn_w_qkv, attn_q_norm, attn_k_norm, attn_w_o, gla_w_in, gla_w_gate_f, gla_b_gate_f, gla_w_gate_b, gla_b_gate_b, gla_head_norm, gla_w_o, final_norm):
    d_model = x_prompt.shape[-1]
    hd = attn_q_norm.shape[-1]
    depth = ffn_norm.shape[0]
    dk_total = gla_w_gate_f.shape[-1]
    rank = gla_w_gate_f.shape[1]
    dv_total = d_model
    gla_heads = dv_total // gla_head_norm.shape[-1]
    assert x_prompt.shape[0] == 1 and x_sample.shape[0] == 1 and 2 * rank <= LANES

    named = sorted([("sample", x_sample[0]), ("prompt", x_prompt[0])], key=lambda a: -a[1].shape[0])
    seqs = [s for _, s in named]
    lens = [s.shape[0] for s in seqs]
    starts = [sum(lens[:i]) for i in range(len(lens))]

    tables = [_rope_tables(n, hd) for n in lens]
    cos = jnp.concatenate([t[0] for t in tables], axis=0)
    sin = jnp.concatenate([t[1] for t in tables], axis=0)

    w13, w2 = ffn_w13, ffn_w2.astype(BF16)
    w_qkv, w_ao, w_go = attn_w_qkv, attn_w_o, gla_w_o
    x, nx = _prep(seqs[0], seqs[1], ffn_norm[0, 0])
    for i in range(depth):
        x, nx = _ffn(x, nx, w13, w2, (i, 0), mix_norm[i], rows_inner=False)
        j = i // 2
        if i % 2 == 0:
            qkv = _qkv_proj(nx, w_qkv, (j,), attn_q_norm[j], attn_k_norm[j], cos, sin, d_model, hd)
            o = jnp.concatenate([_flash_attention(qkv, r0, n, d_model, hd) for r0, n in zip(starts, lens)], axis=0)
            x, nx = _matmul_residual(o, w_ao, (j,), x, 1.0, ffn_norm[i, 1], tm=1024, tn=512, rows_inner=True)
        else:
            n_main = 2 * dk_total + 2 * dv_total
            w_z = jnp.zeros((d_model, LANES), BF16).at[:, :2 * rank].set(gla_w_in[j, :, n_main:].astype(BF16))
            proj, z = _gla_in_proj(nx, gla_w_in, (j,), n_main, w_z)
            wgf = jnp.zeros((LANES, dk_total), BF16).at[:rank].set(gla_w_gate_f[j].astype(BF16))
            wgb = jnp.zeros((LANES, dk_total), BF16).at[rank:2 * rank].set(gla_w_gate_b[j].astype(BF16))
            outs = []
            for r0, n in zip(starts, lens):
                o_f = _gla_direction(proj, z, wgf, gla_b_gate_f[j], r0, n, dk_total, dv_total, gla_heads,
                                     reverse=False)
                outs.append(_gla_direction(proj, z, wgb, gla_b_gate_b[j], r0, n, dk_total, dv_total, gla_heads,
                                           reverse=True, o_fwd=o_f, head_norm=gla_head_norm[j]))
            o = jnp.concatenate(outs, axis=0)
            x, nx = _matmul_residual(o, w_go, (j,), x, 1.0, ffn_norm[i, 1], tm=1024, tn=512, rows_inner=True)
        g_after = ffn_norm[i + 1, 0] if i + 1 < depth else None
        x, nx = _ffn(x, nx, w13, w2, (i, 1), g_after, rows_inner=True)

    outs = {name: _rmsnorm(x, final_norm, F32, r0, n)[None] for (name, _), r0, n in zip(named, starts, lens)}
    return (outs["prompt"], outs["sample"])
```

```python
import collections
import functools
import math

import jax
import jax.numpy as jnp
from jax import lax
from jax.experimental import pallas as pl
from jax.experimental.pallas import tpu as pltpu

GRID_W = 64
ROPE_THETA = 10000.0
GLA_CHUNK = 64
GLA_GATE_NORM = 16.0
FFN_HALF = 0.5
EPS = 1e-6
SOFTMAX_ROWS = 32
FLASH_UNROLL = 32
GLA_STEP_ROWS = 512
NORM_ROWS = 32
LOG2_E = 1.4426950408889634

LANES = 128
V7X_VMEM_LIMIT_BYTES = 60 * 1024 * 1024

F32 = jnp.float32
BF16 = jnp.bfloat16


def _tile(dim, pref, unit=LANES):
    if dim <= pref:
        return dim
    t = (pref // unit) * unit
    while t >= unit:
        if dim % t == 0:
            return t
        t -= unit
    return dim


def _params(*sem):
    return pltpu.CompilerParams(dimension_semantics=sem, vmem_limit_bytes=V7X_VMEM_LIMIT_BYTES)


def _rmsnorm_kernel(x_ref, g_ref, o_ref):
    x = x_ref[...]
    ms = jnp.mean(x * x, axis=-1, keepdims=True)
    o_ref[...] = ((x * lax.rsqrt(ms + EPS)) * g_ref[...]).astype(o_ref.dtype)


def _rmsnorm(x, g, out_dtype, row0, nrows):
    d = x.shape[1]
    tr = _tile(nrows, 256, 8)
    assert row0 % tr == 0
    rb0 = row0 // tr
    return pl.pallas_call(
        _rmsnorm_kernel,
        grid=(nrows // tr,),
        in_specs=[pl.BlockSpec((tr, d), lambda i: (rb0 + i, 0)),
                  pl.BlockSpec((1, d), lambda i: (0, 0))],
        out_specs=pl.BlockSpec((tr, d), lambda i: (i, 0)),
        out_shape=jax.ShapeDtypeStruct((nrows, d), out_dtype),
        compiler_params=_params("parallel"),
        name="rmsnorm",
    )(x, g.reshape(1, d))


def _fold_lanes(y):
    out = y[:, 0:LANES]
    for c in range(1, y.shape[1] // LANES):
        out = out + y[:, c * LANES:(c + 1) * LANES]
    return out


def _normalize_rows(xg_ref, ssq_ref, h_scr, r_scr):
    d = xg_ref.shape[1]
    ms = jnp.sum(ssq_ref[...], axis=-1, keepdims=True) * (1.0 / d)
    r_scr[...] = jnp.broadcast_to(lax.rsqrt(ms + EPS), r_scr.shape)

    def body(c, carry):
        rows = pl.ds(pl.multiple_of(c * NORM_ROWS, NORM_ROWS), NORM_ROWS)
        r = jnp.concatenate([r_scr[rows, :]] * (d // LANES), axis=1)
        h_scr[rows, :] = (xg_ref[rows, :].astype(F32) * r).astype(h_scr.dtype)
        return carry

    lax.fori_loop(0, xg_ref.shape[0] // NORM_ROWS, body, 0)


def _prep_kernel(xa_ref, xb_ref, g_ref, x_ref, xg_ref, ssq_ref, *, nb_a):
    def emit(src_ref):
        x = src_ref[...]
        x_ref[...] = x
        xg_ref[...] = (x * g_ref[...]).astype(xg_ref.dtype)
        ssq_ref[...] = _fold_lanes(x * x)

    @pl.when(pl.program_id(0) < nb_a)
    def _():
        emit(xa_ref)

    @pl.when(pl.program_id(0) >= nb_a)
    def _():
        emit(xb_ref)


def _prep(xa, xb, g):
    (sa, d), sb = xa.shape, xb.shape[0]
    m = sa + sb
    tr = _tile(math.gcd(sa, sb), 256, 8)
    nb_a = sa // tr
    outs = pl.pallas_call(
        functools.partial(_prep_kernel, nb_a=nb_a),
        grid=(m // tr,),
        in_specs=[pl.BlockSpec((tr, d), lambda i: (jnp.minimum(i, nb_a - 1), 0)),
                  pl.BlockSpec((tr, d), lambda i: (jnp.maximum(i - nb_a, 0), 0)),
                  pl.BlockSpec((1, d), lambda i: (0, 0))],
        out_specs=[pl.BlockSpec((tr, d), lambda i: (i, 0)),
                   pl.BlockSpec((tr, d), lambda i: (i, 0)),
                   pl.BlockSpec((tr, LANES), lambda i: (i, 0))],
        out_shape=[jax.ShapeDtypeStruct((m, d), F32), jax.ShapeDtypeStruct((m, d), BF16),
                   jax.ShapeDtypeStruct((m, LANES), F32)],
        compiler_params=_params("arbitrary"),
        name="prenorm_prep",
    )(xa, xb, g.reshape(1, d))
    return outs[0], (outs[1], outs[2])


def _weight_spec(w, idx, tn, col_block):
    k = w.shape[-2]
    return pl.BlockSpec((None,) * len(idx) + (k, tn), lambda i, j: (*idx, 0, col_block(i, j)))


def _mm_residual_kernel(a_ref, w_ref, r_ref, *rest, scale, emit_next, accumulate_ssq):
    acc = jnp.dot(a_ref[...], w_ref[...].astype(BF16), preferred_element_type=F32)
    xn = r_ref[...] + scale * acc
    if not emit_next:
        (o_ref,) = rest
        o_ref[...] = xn
        return
    gn_ref, o_ref, xg_ref, ssq_ref = rest
    o_ref[...] = xn
    xg_ref[...] = (xn * gn_ref[...]).astype(xg_ref.dtype)
    if accumulate_ssq:
        @pl.when(pl.program_id(1) == 0)
        def _():
            ssq_ref[...] = jnp.zeros(ssq_ref.shape, F32)

        ssq_ref[...] += _fold_lanes(xn * xn)
    else:
        ssq_ref[...] = _fold_lanes(xn * xn)


def _matmul_residual(a, w, w_idx, res, scale, g_next, *, tm, tn, rows_inner):
    m, k = a.shape
    n = w.shape[-1]
    tm, tn = _tile(m, tm, 8), _tile(n, tn)
    emit_next = g_next is not None

    def spec(shape, index):
        return pl.BlockSpec(shape, (lambda j, i: index(i, j)) if rows_inner else index)

    in_specs = [spec((tm, k), lambda i, j: (i, 0)),
                spec((None,) * len(w_idx) + (k, tn), lambda i, j: (*w_idx, 0, j)),
                spec((tm, tn), lambda i, j: (i, j))]
    out_specs = [spec((tm, tn), lambda i, j: (i, j))]
    out_shape = [jax.ShapeDtypeStruct((m, n), F32)]
    args = [a, w, res]
    if emit_next:
        ssq_groups = n // tn if rows_inner else 1
        in_specs.append(spec((1, tn), lambda i, j: (0, j)))
        out_specs += [spec((tm, tn), lambda i, j: (i, j)),
                      spec((tm, LANES), (lambda i, j: (i, j)) if rows_inner else (lambda i, j: (i, 0)))]
        out_shape += [jax.ShapeDtypeStruct((m, n), BF16), jax.ShapeDtypeStruct((m, ssq_groups * LANES), F32)]
        args.append(g_next.reshape(1, n))
    outs = pl.pallas_call(
        functools.partial(_mm_residual_kernel, scale=scale, emit_next=emit_next, accumulate_ssq=not rows_inner),
        grid=(n // tn, m // tm) if rows_inner else (m // tm, n // tn),
        in_specs=in_specs,
        out_specs=out_specs,
        out_shape=out_shape,
        compiler_params=_params("parallel", "arbitrary"),
        name="matmul_residual",
    )(*args)
    return (outs[0], (outs[1], outs[2])) if emit_next else (outs[0], None)


def _ffn_up_kernel(xg_ref, ssq_ref, wa_ref, wb_ref, o_ref, h_scr, r_scr):
    @pl.when(pl.program_id(1) == 0)
    def _():
        _normalize_rows(xg_ref, ssq_ref, h_scr, r_scr)

    h = h_scr[...]
    a = jnp.dot(h, wa_ref[...].astype(BF16), preferred_element_type=F32)
    b = jnp.dot(h, wb_ref[...].astype(BF16), preferred_element_type=F32)
    o_ref[...] = ((a * jax.nn.sigmoid(a)) * b).astype(o_ref.dtype)


def _ffn_up(nx, w13, w_idx):
    xg, ssq = nx
    m, k = xg.shape
    f = w13.shape[-1] // 2
    tm, tn = _tile(m, 1024, 8), _tile(f, 256)
    nb = f // tn
    return pl.pallas_call(
        _ffn_up_kernel,
        grid=(m // tm, nb),
        in_specs=[pl.BlockSpec((tm, k), lambda i, j: (i, 0)),
                  pl.BlockSpec((tm, ssq.shape[1]), lambda i, j: (i, 0)),
                  _weight_spec(w13, w_idx, tn, lambda i, j: j),
                  _weight_spec(w13, w_idx, tn, lambda i, j: j + nb)],
        out_specs=pl.BlockSpec((tm, tn), lambda i, j: (i, j)),
        out_shape=jax.ShapeDtypeStruct((m, f), BF16),
        scratch_shapes=[pltpu.VMEM((tm, k), BF16), pltpu.VMEM((tm, LANES), F32)],
        compiler_params=_params("parallel", "arbitrary"),
        name="ffn_up",
    )(xg, ssq, w13, w13)


def _ffn(x, nx, w13, w2, w_idx, g_next, rows_inner):
    u = _ffn_up(nx, w13, w_idx)
    return _matmul_residual(u, w2, w_idx, x, FFN_HALF, g_next, tm=512, tn=512, rows_inner=rows_inner)


def _rope_tables(seq_len, head_dim):
    axis = head_dim // 2
    pos = jnp.arange(seq_len, dtype=jnp.int32)
    row_ids = (pos // GRID_W).astype(F32)
    col_ids = (pos % GRID_W).astype(F32)
    inv_freq = ROPE_THETA ** (-jnp.arange(0, axis, 2, dtype=F32) / axis)
    ang_r = row_ids[:, None] * inv_freq[None, :]
    ang_c = col_ids[:, None] * inv_freq[None, :]
    cos = jnp.concatenate([jnp.cos(ang_r), jnp.cos(ang_r), jnp.cos(ang_c), jnp.cos(ang_c)], axis=-1)
    sin = jnp.concatenate([-jnp.sin(ang_r), jnp.sin(ang_r), -jnp.sin(ang_c), jnp.sin(ang_c)], axis=-1)
    return cos, sin


def _qkv_kernel(xg_ref, ssq_ref, w_ref, gain_ref, cos_ref, sin_ref, o_ref, h_scr, r_scr,
                *, hd, n_q_blocks, n_qk_blocks, q_scale):
    j = pl.program_id(1)

    @pl.when(j == 0)
    def _():
        _normalize_rows(xg_ref, ssq_ref, h_scr, r_scr)

    acc = jnp.dot(h_scr[...], w_ref[...].astype(BF16), preferred_element_type=F32)
    heads = acc.shape[1] // hd
    quarter = hd // 4

    def norm_rope(post_scale):
        cos = cos_ref[...]
        sin = sin_ref[...]
        lane = lax.broadcasted_iota(jnp.int32, cos.shape, 1)
        first_half = (lane % (2 * quarter)) < quarter
        for hh in range(heads):
            x = acc[:, hh * hd:(hh + 1) * hd]
            ms = jnp.mean(x * x, axis=-1, keepdims=True)
            y = (x * lax.rsqrt(ms + EPS)) * gain_ref[:, hh * hd:(hh + 1) * hd]
            partner = jnp.where(first_half, pltpu.roll(y, hd - quarter, 1), pltpu.roll(y, quarter, 1))
            out = y * cos + partner * sin
            if post_scale != 1.0:
                out = out * post_scale
            o_ref[:, hh * hd:(hh + 1) * hd] = out.astype(o_ref.dtype)

    @pl.when(j < n_q_blocks)
    def _():
        norm_rope(q_scale)

    @pl.when(jnp.logical_and(j >= n_q_blocks, j < n_qk_blocks))
    def _():
        norm_rope(1.0)

    @pl.when(j >= n_qk_blocks)
    def _():
        o_ref[...] = acc.astype(o_ref.dtype)


def _qkv_proj(nx, w_qkv, w_idx, q_norm, k_norm, cos, sin, d_model, hd):
    xg, ssq = nx
    m, k = xg.shape
    n = w_qkv.shape[-1]
    kvd = (n - d_model) // 2
    tm = _tile(m, 1024, 8)
    tn = _tile(kvd, 512)
    assert d_model % tn == 0 and tn % hd == 0
    gain = jnp.concatenate([jnp.tile(q_norm, d_model // hd), jnp.tile(k_norm, kvd // hd),
                            jnp.ones((kvd,), F32)]).reshape(1, n)
    kern = functools.partial(_qkv_kernel, hd=hd, n_q_blocks=d_model // tn,
                             n_qk_blocks=(d_model + kvd) // tn, q_scale=hd ** -0.5 * LOG2_E)
    return pl.pallas_call(
        kern,
        grid=(m // tm, n // tn),
        in_specs=[pl.BlockSpec((tm, k), lambda i, j: (i, 0)),
                  pl.BlockSpec((tm, ssq.shape[1]), lambda i, j: (i, 0)),
                  _weight_spec(w_qkv, w_idx, tn, lambda i, j: j),
                  pl.BlockSpec((1, tn), lambda i, j: (0, j)),
                  pl.BlockSpec((tm, hd), lambda i, j: (i, 0)),
                  pl.BlockSpec((tm, hd), lambda i, j: (i, 0))],
        out_specs=pl.BlockSpec((tm, tn), lambda i, j: (i, j)),
        out_shape=jax.ShapeDtypeStruct((m, n), BF16),
        scratch_shapes=[pltpu.VMEM((tm, k), BF16), pltpu.VMEM((tm, LANES), F32)],
        compiler_params=_params("parallel", "arbitrary"),
        name="qkv_proj",
    )(xg, ssq, w_qkv, gain, cos, sin)


def _flash_kernel(q_ref, k_ref, v_ref, o_ref, m_scr, acc_scr, a_scr, p_scr, s_a, s_b,
                  *, group, hd, tk, unroll):
    tq = q_ref.shape[0]
    nk = k_ref.shape[0] // tk
    q = jnp.concatenate([q_ref[:, g * hd:(g + 1) * hd] for g in range(group)], axis=0)
    m_scr[...] = jnp.full(m_scr.shape, -jnp.inf, F32)
    acc_scr[...] = jnp.zeros(acc_scr.shape, F32)

    bufs = (s_a, s_b)

    def scores(c, s_out):
        r0 = pl.multiple_of(c * tk, tk)
        s_out[...] = lax.dot_general(q, k_ref[pl.ds(r0, tk), :], (((1,), (1,)), ((), ())),
                                     preferred_element_type=F32)

    def update(c, s_in):
        r0 = pl.multiple_of(c * tk, tk)
        for rb in range(0, group * tq, SOFTMAX_ROWS):
            rows = slice(rb, rb + SOFTMAX_ROWS)
            s = s_in[rows, :]
            m_prev = m_scr[rows, :]
            m_new = jnp.maximum(m_prev, jnp.max(s, axis=-1, keepdims=True))
            a_scr[rows, :] = jnp.exp2(m_prev - m_new)
            m_scr[rows, :] = m_new
            p_scr[rows, :] = jnp.exp2(s - jnp.concatenate([m_new] * (tk // LANES), axis=1)).astype(BF16)
        v_ext = jnp.concatenate([v_ref[pl.ds(r0, tk), :], jnp.ones((tk, LANES), BF16)], axis=1)
        alpha = a_scr[...]
        acc_scr[...] = (jnp.concatenate([alpha, alpha], axis=1) * acc_scr[...]
                        + jnp.dot(p_scr[...], v_ext, preferred_element_type=F32))

    scores(0, s_a)

    def steady(jj, carry):
        c0 = unroll * jj
        for u in range(unroll):
            scores(c0 + u + 1, bufs[(u + 1) % 2])
            update(c0 + u, bufs[u % 2])
        return carry

    lax.fori_loop(0, nk // unroll - 1, steady, 0)
    c0 = nk - unroll
    for u in range(unroll):
        if u + 1 < unroll:
            scores(c0 + u + 1, bufs[(u + 1) % 2])
        update(c0 + u, bufs[u % 2])
    out = acc_scr[:, 0:hd] / acc_scr[:, hd:2 * hd]
    for g in range(group):
        o_ref[:, g * hd:(g + 1) * hd] = out[g * tq:(g + 1) * tq].astype(o_ref.dtype)


def _flash_attention(qkv, row0, seq, d_model, hd):
    n = qkv.shape[1]
    kvd = (n - d_model) // 2
    n_kv = kvd // hd
    group = d_model // kvd
    tq = _tile(seq, 256, 8)
    tk = _tile(seq // 2, 512)
    nk = seq // tk
    unroll = max(u for u in range(2, FLASH_UNROLL + 1, 2) if nk % u == 0)
    rows = group * tq
    assert hd == LANES and row0 % seq == 0 and row0 % tq == 0 and nk % unroll == 0
    assert rows % SOFTMAX_ROWS == 0
    qb0, sb0 = row0 // tq, row0 // seq
    kc0, vc0 = d_model // hd, (d_model + kvd) // hd
    kern = functools.partial(_flash_kernel, group=group, hd=hd, tk=tk, unroll=unroll)
    return pl.pallas_call(
        kern,
        grid=(n_kv, seq // tq),
        in_specs=[pl.BlockSpec((tq, group * hd), lambda h, i: (qb0 + i, h)),
                  pl.BlockSpec((seq, hd), lambda h, i: (sb0, kc0 + h)),
                  pl.BlockSpec((seq, hd), lambda h, i: (sb0, vc0 + h))],
        out_specs=pl.BlockSpec((tq, group * hd), lambda h, i: (i, h)),
        out_shape=jax.ShapeDtypeStruct((seq, d_model), BF16),
        scratch_shapes=[pltpu.VMEM((rows, LANES), F32),
                        pltpu.VMEM((rows, 2 * hd), F32),
                        pltpu.VMEM((rows, LANES), F32),
                        pltpu.VMEM((rows, tk), BF16),
                        pltpu.VMEM((rows, tk), F32),
                        pltpu.VMEM((rows, tk), F32)],
        compiler_params=_params("parallel", "arbitrary"),
        name="flash_attention",
    )(qkv, qkv, qkv)


def _gla_in_kernel(xg_ref, ssq_ref, w_ref, wz_ref, o_ref, z_ref, h_scr, r_scr):
    @pl.when(pl.program_id(1) == 0)
    def _():
        _normalize_rows(xg_ref, ssq_ref, h_scr, r_scr)
        z_ref[...] = jnp.dot(h_scr[...], wz_ref[...], preferred_element_type=F32)

    o_ref[...] = jnp.dot(h_scr[...], w_ref[...].astype(BF16), preferred_element_type=F32)


def _gla_in_proj(nx, w_in, w_idx, n, w_z):
    xg, ssq = nx
    m, k = xg.shape
    tm, tn = _tile(m, 1024, 8), _tile(n, 512)
    return pl.pallas_call(
        _gla_in_kernel,
        grid=(m // tm, n // tn),
        in_specs=[pl.BlockSpec((tm, k), lambda i, j: (i, 0)),
                  pl.BlockSpec((tm, ssq.shape[1]), lambda i, j: (i, 0)),
                  _weight_spec(w_in, w_idx, tn, lambda i, j: j),
                  pl.BlockSpec((k, LANES), lambda i, j: (0, 0))],
        out_specs=[pl.BlockSpec((tm, tn), lambda i, j: (i, j)),
                   pl.BlockSpec((tm, LANES), lambda i, j: (i, 0))],
        out_shape=[jax.ShapeDtypeStruct((m, n), F32), jax.ShapeDtypeStruct((m, LANES), F32)],
        scratch_shapes=[pltpu.VMEM((tm, k), BF16), pltpu.VMEM((tm, LANES), F32)],
        compiler_params=_params("parallel", "arbitrary"),
        name="gla_in_proj",
    )(xg, ssq, w_in, w_z)


def _log_sigmoid(x):
    return jnp.minimum(x, 0.0) - jnp.log1p(jnp.exp(-jnp.abs(x)))


def _gla_prepare(q_ref, k_ref, v_ref, z_ref, wg_ref, bg_ref, oi_ref, scr, *, reverse, scale):
    c = GLA_CHUNK
    dk = q_ref.shape[1]
    pre = jnp.dot(z_ref[...].astype(BF16), wg_ref[...], preferred_element_type=F32) + bg_ref[...]
    g = _log_sigmoid(pre) / GLA_GATE_NORM
    g1 = g.astype(BF16)
    r1 = g - g1.astype(F32)
    g2 = r1.astype(BF16)
    scr.gp[0] = g1
    scr.gp[1] = g2
    scr.gp[2] = (r1 - g2.astype(F32)).astype(BF16)

    row4 = lax.broadcasted_iota(jnp.int32, (c, 4 * c), 0)
    col4 = lax.broadcasted_iota(jnp.int32, (c, 4 * c), 1)
    src = lax.rem(col4, c)
    tri4 = jnp.logical_and((row4 <= src) if reverse else (row4 >= src), col4 < 3 * c).astype(BF16)
    zeros = jnp.zeros((c, dk), BF16)
    for ci in range(q_ref.shape[0] // c):
        rows = slice(ci * c, (ci + 1) * c)
        b = jnp.dot(tri4, jnp.concatenate([scr.gp[0, rows, :], scr.gp[1, rows, :], scr.gp[2, rows, :], zeros],
                                          axis=0), preferred_element_type=F32)
        b_tot = b[0:1, :] if reverse else b[c - 1:c, :]
        scr.b[rows, :] = b
        scr.bt[rows, :] = jnp.broadcast_to(b_tot, (c, dk))
        scr.dec[ci] = jnp.exp(jnp.transpose(jnp.broadcast_to(b_tot, (LANES, dk))))

    b = scr.b[...]
    k = k_ref[...]
    scr.qd[...] = ((q_ref[...] * scale) * jnp.exp(b)).astype(BF16)
    scr.kd[...] = (k * jnp.exp(-b)).astype(BF16)
    scr.ke[...] = (k * jnp.exp(scr.bt[...] - b)).astype(BF16)

    row = lax.broadcasted_iota(jnp.int32, (c, c), 0)
    col = lax.broadcasted_iota(jnp.int32, (c, c), 1)
    tri = (row <= col) if reverse else (row >= col)
    for ci in range(q_ref.shape[0] // c):
        rows = slice(ci * c, (ci + 1) * c)
        att = lax.dot_general(scr.qd[rows, :], scr.kd[rows, :], (((1,), (1,)), ((), ())),
                              preferred_element_type=F32)
        scr.att[rows, :] = jnp.where(tri, att, 0.0).astype(BF16)
    for ci in range(q_ref.shape[0] // c):
        rows = slice(ci * c, (ci + 1) * c)
        oi_ref[rows, :] = jnp.dot(scr.att[rows, :], v_ref[rows, :].astype(BF16), preferred_element_type=F32)


def _gla_chunk(ci, v_ref, scr, state_ref):
    rows = slice(ci * GLA_CHUNK, (ci + 1) * GLA_CHUNK)
    st = state_ref[...]
    o = jnp.dot(scr.qd[rows, :], st.astype(BF16), preferred_element_type=F32)
    decay = jnp.concatenate([scr.dec[ci]] * (st.shape[1] // LANES), axis=1)
    state_ref[...] = st * decay + lax.dot_general(
        scr.ke[rows, :], v_ref[rows, :].astype(BF16), (((0,), (0,)), ((), ())), preferred_element_type=F32)
    return rows, o


_GlaScratch = collections.namedtuple("_GlaScratch", "gp b bt qd kd ke dec att")


def _gla_scratch_shapes(rows, dk):
    return _GlaScratch(
        gp=pltpu.VMEM((3, rows, dk), BF16),
        b=pltpu.VMEM((rows, dk), F32),
        bt=pltpu.VMEM((rows, dk), F32),
        qd=pltpu.VMEM((rows, dk), BF16),
        kd=pltpu.VMEM((rows, dk), BF16),
        ke=pltpu.VMEM((rows, dk), BF16),
        dec=pltpu.VMEM((rows // GLA_CHUNK, dk, LANES), F32),
        att=pltpu.VMEM((rows, GLA_CHUNK), BF16))


def _gla_fwd_kernel(q_ref, k_ref, v_ref, z_ref, wg_ref, bg_ref, o_ref, state_ref, *scratch, scale):
    scr = _GlaScratch(*scratch)

    @pl.when(pl.program_id(1) == 0)
    def _():
        state_ref[...] = jnp.zeros(state_ref.shape, F32)

    _gla_prepare(q_ref, k_ref, v_ref, z_ref, wg_ref, bg_ref, o_ref, scr, reverse=False, scale=scale)
    for ci in range(q_ref.shape[0] // GLA_CHUNK):
        rows, o_inter = _gla_chunk(ci, v_ref, scr, state_ref)
        o_ref[rows, :] += o_inter


def _gla_bwd_kernel(q_ref, k_ref, v_ref, z_ref, wg_ref, bg_ref, of_ref, r_ref, hn_ref, o_ref, state_ref,
                    oi_scr, *scratch, scale):
    scr = _GlaScratch(*scratch)

    @pl.when(pl.program_id(1) == 0)
    def _():
        state_ref[...] = jnp.zeros(state_ref.shape, F32)

    _gla_prepare(q_ref, k_ref, v_ref, z_ref, wg_ref, bg_ref, oi_scr, scr, reverse=True, scale=scale)
    for ci in reversed(range(q_ref.shape[0] // GLA_CHUNK)):
        rows, o_inter = _gla_chunk(ci, v_ref, scr, state_ref)
        o = of_ref[rows, :] + (oi_scr[rows, :] + o_inter)
        ms = jnp.mean(o * o, axis=-1, keepdims=True)
        on = (o * lax.rsqrt(ms + EPS)) * hn_ref[...]
        r = r_ref[rows, :]
        o_ref[rows, :] = (on * (r * jax.nn.sigmoid(r))).astype(o_ref.dtype)


def _gla_direction(proj, z, wg_pad, bg, row0, seq, dk_total, dv_total, heads, *, reverse,
                   o_fwd=None, head_norm=None):
    dk, dv = dk_total // heads, dv_total // heads
    rows = _tile(seq, GLA_STEP_ROWS, GLA_CHUNK)
    nchunks = rows // GLA_CHUNK
    nb = seq // rows
    assert row0 % rows == 0 and (2 * dk_total) % dv == 0
    rb0 = row0 // rows
    kc0 = dk_total // dk
    vc0 = (2 * dk_total) // dv
    rc0 = (2 * dk_total + dv_total) // dv
    if reverse:
        rblk = lambda i: rb0 + nb - 1 - i
        oblk = lambda i: nb - 1 - i
    else:
        rblk = lambda i: rb0 + i
        oblk = lambda i: i
    in_specs = [pl.BlockSpec((rows, dk), lambda h, i: (rblk(i), h)),
                pl.BlockSpec((rows, dk), lambda h, i: (rblk(i), kc0 + h)),
                pl.BlockSpec((rows, dv), lambda h, i: (rblk(i), vc0 + h)),
                pl.BlockSpec((rows, LANES), lambda h, i: (rblk(i), 0)),
                pl.BlockSpec((LANES, dk), lambda h, i: (0, h)),
                pl.BlockSpec((1, dk), lambda h, i: (0, h))]
    args = [proj, proj, proj, z, wg_pad, bg.reshape(1, dk_total)]
    scale = dk ** -0.5
    if reverse:
        in_specs += [pl.BlockSpec((rows, dv), lambda h, i: (oblk(i), h)),
                     pl.BlockSpec((rows, dv), lambda h, i: (rblk(i), rc0 + h)),
                     pl.BlockSpec((1, dv), lambda h, i: (0, 0))]
        args += [o_fwd, proj, head_norm.reshape(1, dv)]
        kern = functools.partial(_gla_bwd_kernel, scale=scale)
        out_dtype = BF16
    else:
        kern = functools.partial(_gla_fwd_kernel, scale=scale)
        out_dtype = F32
    scratch = [pltpu.VMEM((dk, dv), F32)]
    if reverse:
        scratch.append(pltpu.VMEM((rows, dv), F32))
    scratch += list(_gla_scratch_shapes(rows, dk))
    return pl.pallas_call(
        kern,
        grid=(heads, nb),
        in_specs=in_specs,
        out_specs=pl.BlockSpec((rows, dv), lambda h, i: (oblk(i), h)),
        out_shape=jax.ShapeDtypeStruct((seq, dv_total), out_dtype),
        scratch_shapes=scratch,
        compiler_params=_params("parallel", "arbitrary"),
        name="gla_bwd" if reverse else "gla_fwd",
    )(*args)


def kernel(x_prompt, x_sample, ffn_norm, ffn_w13, ffn_w2, mix_norm, attn_w_qkv, attn_q_norm, attn_k_norm, attn_w_o, gla_w_in, gla_w_gate_f, gla_b_gate_f, gla_w_gate_b, gla_b_gate_b, gla_head_norm, gla_w_o, final_norm):
    d_model = x_prompt.shape[-1]
    hd = attn_q_norm.shape[-1]
    depth = ffn_norm.shape[0]
    dk_total = gla_w_gate_f.shape[-1]
    rank = gla_w_gate_f.shape[1]
    dv_total = d_model
    gla_heads = dv_total // gla_head_norm.shape[-1]
    assert x_prompt.shape[0] == 1 and x_sample.shape[0] == 1 and 2 * rank <= LANES

    named = sorted([("sample", x_sample[0]), ("prompt", x_prompt[0])], key=lambda a: -a[1].shape[0])
    seqs = [s for _, s in named]
    lens = [s.shape[0] for s in seqs]
    starts = [sum(lens[:i]) for i in range(len(lens))]

    tables = [_rope_tables(n, hd) for n in lens]
    cos = jnp.concatenate([t[0] for t in tables], axis=0)
    sin = jnp.concatenate([t[1] for t in tables], axis=0)

    w13, w2 = ffn_w13, ffn_w2.astype(BF16)
    w_qkv, w_ao, w_go = attn_w_qkv, attn_w_o, gla_w_o
    x, nx = _prep(seqs[0], seqs[1], ffn_norm[0, 0])
    for i in range(depth):
        x, nx = _ffn(x, nx, w13, w2, (i, 0), mix_norm[i], rows_inner=True)
        j = i // 2
        if i % 2 == 0:
            qkv = _qkv_proj(nx, w_qkv, (j,), attn_q_norm[j], attn_k_norm[j], cos, sin, d_model, hd)
            o = jnp.concatenate([_flash_attention(qkv, r0, n, d_model, hd) for r0, n in zip(starts, lens)], axis=0)
            x, nx = _matmul_residual(o, w_ao, (j,), x, 1.0, ffn_norm[i, 1], tm=1024, tn=512, rows_inner=True)
        else:
            n_main = 2 * dk_total + 2 * dv_total
            w_z = jnp.zeros((d_model, LANES), BF16).at[:, :2 * rank].set(gla_w_in[j, :, n_main:].astype(BF16))
            proj, z = _gla_in_proj(nx, gla_w_in, (j,), n_main, w_z)
            wgf = jnp.zeros((LANES, dk_total), BF16).at[:rank].set(gla_w_gate_f[j].astype(BF16))
            wgb = jnp.zeros((LANES, dk_total), BF16).at[rank:2 * rank].set(gla_w_gate_b[j].astype(BF16))
            outs = []
            for r0, n in zip(starts, lens):
                o_f = _gla_direction(proj, z, wgf, gla_b_gate_f[j], r0, n, dk_total, dv_total, gla_heads,
                                     reverse=False)
                outs.append(_gla_direction(proj, z, wgb, gla_b_gate_b[j], r0, n, dk_total, dv_total, gla_heads,
                                           reverse=True, o_fwd=o_f, head_norm=gla_head_norm[j]))
            o = jnp.concatenate(outs, axis=0)
            x, nx = _matmul_residual(o, w_go, (j,), x, 1.0, ffn_norm[i, 1], tm=1024, tn=512, rows_inner=True)
        g_after = ffn_norm[i + 1, 0] if i + 1 < depth else None
        x, nx = _ffn(x, nx, w13, w2, (i, 1), g_after, rows_inner=True)

    outs = {name: _rmsnorm(x, final_norm, F32, r0, n)[None] for (name, _), r0, n in zip(named, starts, lens)}
    return (outs["prompt"], outs["sample"])
```

```python
import collections
import functools
import math

import jax
import jax.numpy as jnp
from jax import lax
from jax.experimental import pallas as pl
from jax.experimental.pallas import tpu as pltpu

GRID_W = 64
ROPE_THETA = 10000.0
GLA_CHUNK = 64
GLA_GATE_NORM = 16.0
FFN_HALF = 0.5
EPS = 1e-6
SOFTMAX_ROWS = 32
FLASH_UNROLL = 32
GLA_STEP_ROWS = 512
NORM_ROWS = 32
LOG2_E = 1.4426950408889634

LANES = 128
V7X_VMEM_LIMIT_BYTES = 60 * 1024 * 1024

F32 = jnp.float32
BF16 = jnp.bfloat16


def _tile(dim, pref, unit=LANES):
    if dim <= pref:
        return dim
    t = (pref // unit) * unit
    while t >= unit:
        if dim % t == 0:
            return t
        t -= unit
    return dim


def _params(*sem):
    return pltpu.CompilerParams(dimension_semantics=sem, vmem_limit_bytes=V7X_VMEM_LIMIT_BYTES)


def _rmsnorm_kernel(x_ref, g_ref, o_ref):
    x = x_ref[...]
    ms = jnp.mean(x * x, axis=-1, keepdims=True)
    o_ref[...] = ((x * lax.rsqrt(ms + EPS)) * g_ref[...]).astype(o_ref.dtype)


def _rmsnorm(x, g, out_dtype, row0, nrows):
    d = x.shape[1]
    tr = _tile(nrows, 256, 8)
    assert row0 % tr == 0
    rb0 = row0 // tr
    return pl.pallas_call(
        _rmsnorm_kernel,
        grid=(nrows // tr,),
        in_specs=[pl.BlockSpec((tr, d), lambda i: (rb0 + i, 0)),
                  pl.BlockSpec((1, d), lambda i: (0, 0))],
        out_specs=pl.BlockSpec((tr, d), lambda i: (i, 0)),
        out_shape=jax.ShapeDtypeStruct((nrows, d), out_dtype),
        compiler_params=_params("parallel"),
        name="rmsnorm",
    )(x, g.reshape(1, d))


def _fold_lanes(y):
    out = y[:, 0:LANES]
    for c in range(1, y.shape[1] // LANES):
        out = out + y[:, c * LANES:(c + 1) * LANES]
    return out


def _normalize_rows(xg_ref, ssq_ref, h_scr, r_scr):
    d = xg_ref.shape[1]
    ms = jnp.sum(ssq_ref[...], axis=-1, keepdims=True) * (1.0 / d)
    r_scr[...] = jnp.broadcast_to(lax.rsqrt(ms + EPS), r_scr.shape)

    def body(c, carry):
        rows = pl.ds(pl.multiple_of(c * NORM_ROWS, NORM_ROWS), NORM_ROWS)
        r = jnp.concatenate([r_scr[rows, :]] * (d // LANES), axis=1)
        h_scr[rows, :] = (xg_ref[rows, :].astype(F32) * r).astype(h_scr.dtype)
        return carry

    lax.fori_loop(0, xg_ref.shape[0] // NORM_ROWS, body, 0)


def _prep_kernel(xa_ref, xb_ref, g_ref, x_ref, xg_ref, ssq_ref, *, nb_a):
    def emit(src_ref):
        x = src_ref[...]
        x_ref[...] = x
        xg_ref[...] = (x * g_ref[...]).astype(xg_ref.dtype)
        ssq_ref[...] = _fold_lanes(x * x)

    @pl.when(pl.program_id(0) < nb_a)
    def _():
        emit(xa_ref)

    @pl.when(pl.program_id(0) >= nb_a)
    def _():
        emit(xb_ref)


def _prep(xa, xb, g):
    (sa, d), sb = xa.shape, xb.shape[0]
    m = sa + sb
    tr = _tile(math.gcd(sa, sb), 256, 8)
    nb_a = sa // tr
    outs = pl.pallas_call(
        functools.partial(_prep_kernel, nb_a=nb_a),
        grid=(m // tr,),
        in_specs=[pl.BlockSpec((tr, d), lambda i: (jnp.minimum(i, nb_a - 1), 0)),
                  pl.BlockSpec((tr, d), lambda i: (jnp.maximum(i - nb_a, 0), 0)),
                  pl.BlockSpec((1, d), lambda i: (0, 0))],
        out_specs=[pl.BlockSpec((tr, d), lambda i: (i, 0)),
                   pl.BlockSpec((tr, d), lambda i: (i, 0)),
                   pl.BlockSpec((tr, LANES), lambda i: (i, 0))],
        out_shape=[jax.ShapeDtypeStruct((m, d), F32), jax.ShapeDtypeStruct((m, d), BF16),
                   jax.ShapeDtypeStruct((m, LANES), F32)],
        compiler_params=_params("arbitrary"),
        name="prenorm_prep",
    )(xa, xb, g.reshape(1, d))
    return outs[0], (outs[1], outs[2])


def _weight_spec(w, idx, tn, col_block):
    k = w.shape[-2]
    return pl.BlockSpec((None,) * len(idx) + (k, tn), lambda i, j: (*idx, 0, col_block(i, j)))


def _mm_residual_kernel(a_ref, w_ref, r_ref, *rest, scale, emit_next, accumulate_ssq):
    acc = jnp.dot(a_ref[...], w_ref[...].astype(BF16), preferred_element_type=F32)
    xn = r_ref[...] + scale * acc
    if not emit_next:
        (o_ref,) = rest
        o_ref[...] = xn
        return
    gn_ref, o_ref, xg_ref, ssq_ref = rest
    o_ref[...] = xn
    xg_ref[...] = (xn * gn_ref[...]).astype(xg_ref.dtype)
    if accumulate_ssq:
        @pl.when(pl.program_id(1) == 0)
        def _():
            ssq_ref[...] = jnp.zeros(ssq_ref.shape, F32)

        ssq_ref[...] += _fold_lanes(xn * xn)
    else:
        ssq_ref[...] = _fold_lanes(xn * xn)


def _matmul_residual(a, w, w_idx, res, scale, g_next, *, tm, tn, rows_inner):
    m, k = a.shape
    n = w.shape[-1]
    tm, tn = _tile(m, tm, 8), _tile(n, tn)
    emit_next = g_next is not None

    def spec(shape, index):
        return pl.BlockSpec(shape, (lambda j, i: index(i, j)) if rows_inner else index)

    in_specs = [spec((tm, k), lambda i, j: (i, 0)),
                spec((None,) * len(w_idx) + (k, tn), lambda i, j: (*w_idx, 0, j)),
                spec((tm, tn), lambda i, j: (i, j))]
    out_specs = [spec((tm, tn), lambda i, j: (i, j))]
    out_shape = [jax.ShapeDtypeStruct((m, n), F32)]
    args = [a, w, res]
    if emit_next:
        ssq_groups = n // tn if rows_inner else 1
        in_specs.append(spec((1, tn), lambda i, j: (0, j)))
        out_specs += [spec((tm, tn), lambda i, j: (i, j)),
                      spec((tm, LANES), (lambda i, j: (i, j)) if rows_inner else (lambda i, j: (i, 0)))]
        out_shape += [jax.ShapeDtypeStruct((m, n), BF16), jax.ShapeDtypeStruct((m, ssq_groups * LANES), F32)]
        args.append(g_next.reshape(1, n))
    outs = pl.pallas_call(
        functools.partial(_mm_residual_kernel, scale=scale, emit_next=emit_next, accumulate_ssq=not rows_inner),
        grid=(n // tn, m // tm) if rows_inner else (m // tm, n // tn),
        in_specs=in_specs,
        out_specs=out_specs,
        out_shape=out_shape,
        compiler_params=_params("parallel", "arbitrary"),
        name="matmul_residual",
    )(*args)
    return (outs[0], (outs[1], outs[2])) if emit_next else (outs[0], None)


def _ffn_up_kernel(xg_ref, ssq_ref, wa_ref, wb_ref, o_ref, h_scr, r_scr):
    @pl.when(pl.program_id(1) == 0)
    def _():
        _normalize_rows(xg_ref, ssq_ref, h_scr, r_scr)

    h = h_scr[...]
    a = jnp.dot(h, wa_ref[...].astype(BF16), preferred_element_type=F32)
    b = jnp.dot(h, wb_ref[...].astype(BF16), preferred_element_type=F32)
    o_ref[...] = ((a * jax.nn.sigmoid(a)) * b).astype(o_ref.dtype)


def _ffn_up(nx, w13, w_idx):
    xg, ssq = nx
    m, k = xg.shape
    f = w13.shape[-1] // 2
    tm, tn = _tile(m, 1024, 8), _tile(f, 256)
    nb = f // tn
    return pl.pallas_call(
        _ffn_up_kernel,
        grid=(m // tm, nb),
        in_specs=[pl.BlockSpec((tm, k), lambda i, j: (i, 0)),
                  pl.BlockSpec((tm, ssq.shape[1]), lambda i, j: (i, 0)),
                  _weight_spec(w13, w_idx, tn, lambda i, j: j),
                  _weight_spec(w13, w_idx, tn, lambda i, j: j + nb)],
        out_specs=pl.BlockSpec((tm, tn), lambda i, j: (i, j)),
        out_shape=jax.ShapeDtypeStruct((m, f), BF16),
        scratch_shapes=[pltpu.VMEM((tm, k), BF16), pltpu.VMEM((tm, LANES), F32)],
        compiler_params=_params("parallel", "arbitrary"),
        name="ffn_up",
    )(xg, ssq, w13, w13)


def _ffn(x, nx, w13, w2, w_idx, g_next, rows_inner):
    u = _ffn_up(nx, w13, w_idx)
    return _matmul_residual(u, w2, w_idx, x, FFN_HALF, g_next, tm=512, tn=512, rows_inner=rows_inner)


def _rope_tables(seq_len, head_dim):
    axis = head_dim // 2
    pos = jnp.arange(seq_len, dtype=jnp.int32)
    row_ids = (pos // GRID_W).astype(F32)
    col_ids = (pos % GRID_W).astype(F32)
    inv_freq = ROPE_THETA ** (-jnp.arange(0, axis, 2, dtype=F32) / axis)
    ang_r = row_ids[:, None] * inv_freq[None, :]
    ang_c = col_ids[:, None] * inv_freq[None, :]
    cos = jnp.concatenate([jnp.cos(ang_r), jnp.cos(ang_r), jnp.cos(ang_c), jnp.cos(ang_c)], axis=-1)
    sin = jnp.concatenate([-jnp.sin(ang_r), jnp.sin(ang_r), -jnp.sin(ang_c), jnp.sin(ang_c)], axis=-1)
    return cos, sin


def _qkv_kernel(xg_ref, ssq_ref, w_ref, gain_ref, cos_ref, sin_ref, o_ref, h_scr, r_scr,
                *, hd, n_q_blocks, n_qk_blocks, q_scale):
    j = pl.program_id(1)

    @pl.when(j == 0)
    def _():
        _normalize_rows(xg_ref, ssq_ref, h_scr, r_scr)

    acc = jnp.dot(h_scr[...], w_ref[...].astype(BF16), preferred_element_type=F32)
    heads = acc.shape[1] // hd
    quarter = hd // 4

    def norm_rope(post_scale):
        cos = cos_ref[...]
        sin = sin_ref[...]
        lane = lax.broadcasted_iota(jnp.int32, cos.shape, 1)
        first_half = (lane % (2 * quarter)) < quarter
        for hh in range(heads):
            x = acc[:, hh * hd:(hh + 1) * hd]
            ms = jnp.mean(x * x, axis=-1, keepdims=True)
            y = (x * lax.rsqrt(ms + EPS)) * gain_ref[:, hh * hd:(hh + 1) * hd]
            partner = jnp.where(first_half, pltpu.roll(y, hd - quarter, 1), pltpu.roll(y, quarter, 1))
            out = y * cos + partner * sin
            if post_scale != 1.0:
                out = out * post_scale
            o_ref[:, hh * hd:(hh + 1) * hd] = out.astype(o_ref.dtype)

    @pl.when(j < n_q_blocks)
    def _():
        norm_rope(q_scale)

    @pl.when(jnp.logical_and(j >= n_q_blocks, j < n_qk_blocks))
    def _():
        norm_rope(1.0)

    @pl.when(j >= n_qk_blocks)
    def _():
        o_ref[...] = acc.astype(o_ref.dtype)


def _qkv_proj(nx, w_qkv, w_idx, q_norm, k_norm, cos, sin, d_model, hd):
    xg, ssq = nx
    m, k = xg.shape
    n = w_qkv.shape[-1]
    kvd = (n - d_model) // 2
    tm = _tile(m, 1024, 8)
    tn = _tile(kvd, 512)
    assert d_model % tn == 0 and tn % hd == 0
    gain = jnp.concatenate([jnp.tile(q_norm, d_model // hd), jnp.tile(k_norm, kvd // hd),
                            jnp.ones((kvd,), F32)]).reshape(1, n)
    kern = functools.partial(_qkv_kernel, hd=hd, n_q_blocks=d_model // tn,
                             n_qk_blocks=(d_model + kvd) // tn, q_scale=hd ** -0.5 * LOG2_E)
    return pl.pallas_call(
        kern,
        grid=(m // tm, n // tn),
        in_specs=[pl.BlockSpec((tm, k), lambda i, j: (i, 0)),
                  pl.BlockSpec((tm, ssq.shape[1]), lambda i, j: (i, 0)),
                  _weight_spec(w_qkv, w_idx, tn, lambda i, j: j),
                  pl.BlockSpec((1, tn), lambda i, j: (0, j)),
                  pl.BlockSpec((tm, hd), lambda i, j: (i, 0)),
                  pl.BlockSpec((tm, hd), lambda i, j: (i, 0))],
        out_specs=pl.BlockSpec((tm, tn), lambda i, j: (i, j)),
        out_shape=jax.ShapeDtypeStruct((m, n), BF16),
        scratch_shapes=[pltpu.VMEM((tm, k), BF16), pltpu.VMEM((tm, LANES), F32)],
        compiler_params=_params("parallel", "arbitrary"),
        name="qkv_proj",
    )(xg, ssq, w_qkv, gain, cos, sin)


def _flash_kernel(q_ref, k_ref, v_ref, o_ref, m_scr, acc_scr, a_scr, p_scr, s_a, s_b,
                  *, group, hd, tk, unroll):
    tq = q_ref.shape[0]
    nk = k_ref.shape[0] // tk
    q = jnp.concatenate([q_ref[:, g * hd:(g + 1) * hd] for g in range(group)], axis=0)
    m_scr[...] = jnp.full(m_scr.shape, -jnp.inf, F32)
    acc_scr[...] = jnp.zeros(acc_scr.shape, F32)

    bufs = (s_a, s_b)

    def scores(c, s_out):
        r0 = pl.multiple_of(c * tk, tk)
        s_out[...] = lax.dot_general(q, k_ref[pl.ds(r0, tk), :], (((1,), (1,)), ((), ())),
                                     preferred_element_type=F32)

    def update(c, s_in):
        r0 = pl.multiple_of(c * tk, tk)
        for rb in range(0, group * tq, SOFTMAX_ROWS):
            rows = slice(rb, rb + SOFTMAX_ROWS)
            s = s_in[rows, :]
            m_prev = m_scr[rows, :]
            m_new = jnp.maximum(m_prev, jnp.max(s, axis=-1, keepdims=True))
            a_scr[rows, :] = jnp.exp2(m_prev - m_new)
            m_scr[rows, :] = m_new
            p_scr[rows, :] = jnp.exp2(s - jnp.concatenate([m_new] * (tk // LANES), axis=1)).astype(BF16)
        v_ext = jnp.concatenate([v_ref[pl.ds(r0, tk), :], jnp.ones((tk, LANES), BF16)], axis=1)
        alpha = a_scr[...]
        acc_scr[...] = (jnp.concatenate([alpha, alpha], axis=1) * acc_scr[...]
                        + jnp.dot(p_scr[...], v_ext, preferred_element_type=F32))

    scores(0, s_a)

    def steady(jj, carry):
        c0 = unroll * jj
        for u in range(unroll):
            scores(c0 + u + 1, bufs[(u + 1) % 2])
            update(c0 + u, bufs[u % 2])
        return carry

    lax.fori_loop(0, nk // unroll - 1, steady, 0)
    c0 = nk - unroll
    for u in range(unroll):
        if u + 1 < unroll:
            scores(c0 + u + 1, bufs[(u + 1) % 2])
        update(c0 + u, bufs[u % 2])
    out = acc_scr[:, 0:hd] / acc_scr[:, hd:2 * hd]
    for g in range(group):
        o_ref[:, g * hd:(g + 1) * hd] = out[g * tq:(g + 1) * tq].astype(o_ref.dtype)


def _flash_attention(qkv, row0, seq, d_model, hd):
    n = qkv.shape[1]
    kvd = (n - d_model) // 2
    n_kv = kvd // hd
    group = d_model // kvd
    tq = _tile(seq, 256, 8)
    tk = _tile(seq // 2, 512)
    nk = seq // tk
    unroll = max(u for u in range(2, FLASH_UNROLL + 1, 2) if nk % u == 0)
    rows = group * tq
    assert hd == LANES and row0 % seq == 0 and row0 % tq == 0 and nk % unroll == 0
    assert rows % SOFTMAX_ROWS == 0
    qb0, sb0 = row0 // tq, row0 // seq
    kc0, vc0 = d_model // hd, (d_model + kvd) // hd
    kern = functools.partial(_flash_kernel, group=group, hd=hd, tk=tk, unroll=unroll)
    return pl.pallas_call(
        kern,
        grid=(n_kv, seq // tq),
        in_specs=[pl.BlockSpec((tq, group * hd), lambda h, i: (qb0 + i, h)),
                  pl.BlockSpec((seq, hd), lambda h, i: (sb0, kc0 + h)),
                  pl.BlockSpec((seq, hd), lambda h, i: (sb0, vc0 + h))],
        out_specs=pl.BlockSpec((tq, group * hd), lambda h, i: (i, h)),
        out_shape=jax.ShapeDtypeStruct((seq, d_model), BF16),
        scratch_shapes=[pltpu.VMEM((rows, LANES), F32),
                        pltpu.VMEM((rows, 2 * hd), F32),
                        pltpu.VMEM((rows, LANES), F32),
                        pltpu.VMEM((rows, tk), BF16),
                        pltpu.VMEM((rows, tk), F32),
                        pltpu.VMEM((rows, tk), F32)],
        compiler_params=_params("parallel", "arbitrary"),
        name="flash_attention",
    )(qkv, qkv, qkv)


def _gla_in_kernel(xg_ref, ssq_ref, w_ref, wz_ref, o_ref, z_ref, h_scr, r_scr):
    @pl.when(pl.program_id(1) == 0)
    def _():
        _normalize_rows(xg_ref, ssq_ref, h_scr, r_scr)
        z_ref[...] = jnp.dot(h_scr[...], wz_ref[...], preferred_element_type=F32)

    o_ref[...] = jnp.dot(h_scr[...], w_ref[...].astype(BF16), preferred_element_type=F32)


def _gla_in_proj(nx, w_in, w_idx, n, w_z):
    xg, ssq = nx
    m, k = xg.shape
    tm, tn = _tile(m, 1024, 8), _tile(n, 512)
    return pl.pallas_call(
        _gla_in_kernel,
        grid=(m // tm, n // tn),
        in_specs=[pl.BlockSpec((tm, k), lambda i, j: (i, 0)),
                  pl.BlockSpec((tm, ssq.shape[1]), lambda i, j: (i, 0)),
                  _weight_spec(w_in, w_idx, tn, lambda i, j: j),
                  pl.BlockSpec((k, LANES), lambda i, j: (0, 0))],
        out_specs=[pl.BlockSpec((tm, tn), lambda i, j: (i, j)),
                   pl.BlockSpec((tm, LANES), lambda i, j: (i, 0))],
        out_shape=[jax.ShapeDtypeStruct((m, n), F32), jax.ShapeDtypeStruct((m, LANES), F32)],
        scratch_shapes=[pltpu.VMEM((tm, k), BF16), pltpu.VMEM((tm, LANES), F32)],
        compiler_params=_params("parallel", "arbitrary"),
        name="gla_in_proj",
    )(xg, ssq, w_in, w_z)


def _log_sigmoid(x):
    return jnp.minimum(x, 0.0) - jnp.log1p(jnp.exp(-jnp.abs(x)))


def _gla_prepare(q_ref, k_ref, v_ref, z_ref, wg_ref, bg_ref, oi_ref, scr, *, reverse, scale):
    c = GLA_CHUNK
    dk = q_ref.shape[1]
    pre = jnp.dot(z_ref[...].astype(BF16), wg_ref[...], preferred_element_type=F32) + bg_ref[...]
    g = _log_sigmoid(pre) / GLA_GATE_NORM
    g1 = g.astype(BF16)
    r1 = g - g1.astype(F32)
    g2 = r1.astype(BF16)
    scr.gp[0] = g1
    scr.gp[1] = g2
    scr.gp[2] = (r1 - g2.astype(F32)).astype(BF16)

    row4 = lax.broadcasted_iota(jnp.int32, (c, 4 * c), 0)
    col4 = lax.broadcasted_iota(jnp.int32, (c, 4 * c), 1)
    src = lax.rem(col4, c)
    tri4 = jnp.logical_and((row4 <= src) if reverse else (row4 >= src), col4 < 3 * c).astype(BF16)
    zeros = jnp.zeros((c, dk), BF16)
    zeros_f32 = jnp.zeros((c, dk), F32)

    def cumulative(rows):
        b = jnp.dot(tri4, jnp.concatenate([scr.gp[0, rows, :], scr.gp[1, rows, :], scr.gp[2, rows, :], zeros],
                                          axis=0), preferred_element_type=F32)
        return b, (b[0:1, :] if reverse else b[c - 1:c, :])

    for first, second in _gla_pairs(q_ref.shape[0] // c, reverse):
        b1, t1 = cumulative(first)
        b2, t2 = cumulative(second)
        scr.b[first, :] = b1
        scr.b[second, :] = b2
        scr.bt[first, :] = jnp.broadcast_to(t1, (c, dk))
        scr.bt[second, :] = jnp.broadcast_to(t2, (c, dk))
        scr.pq[first, :] = zeros_f32
        scr.pq[second, :] = jnp.broadcast_to(t1, (c, dk))
        scr.pk[first, :] = jnp.broadcast_to(t2, (c, dk))
        scr.pk[second, :] = zeros_f32
        scr.dec[min(first.start, second.start) // (2 * c)] = jnp.exp(
            jnp.transpose(jnp.broadcast_to(t1 + t2, (LANES, dk))))

    b = scr.b[...]
    k = k_ref[...]
    qs = q_ref[...] * scale
    scr.qd[...] = (qs * jnp.exp(b)).astype(BF16)
    scr.qr[...] = (qs * jnp.exp(b + scr.pq[...])).astype(BF16)
    scr.kd[...] = (k * jnp.exp(-b)).astype(BF16)
    tail = scr.bt[...] - b
    scr.ke[...] = (k * jnp.exp(tail)).astype(BF16)
    scr.ku[...] = (k * jnp.exp(tail + scr.pk[...])).astype(BF16)

    row = lax.broadcasted_iota(jnp.int32, (c, c), 0)
    col = lax.broadcasted_iota(jnp.int32, (c, c), 1)
    tri = (row <= col) if reverse else (row >= col)
    nt = (((1,), (1,)), ((), ()))
    for first, second in _gla_pairs(q_ref.shape[0] // c, reverse):
        for rows in (first, second):
            att = lax.dot_general(scr.qd[rows, :], scr.kd[rows, :], nt, preferred_element_type=F32)
            scr.att[rows, :] = jnp.where(tri, att, 0.0).astype(BF16)
        scr.atx[second, :] = lax.dot_general(scr.qd[second, :], scr.ke[first, :], nt,
                                             preferred_element_type=F32).astype(BF16)
    for first, second in _gla_pairs(q_ref.shape[0] // c, reverse):
        v1 = v_ref[first, :].astype(BF16)
        oi_ref[first, :] = jnp.dot(scr.att[first, :], v1, preferred_element_type=F32)
        oi_ref[second, :] = (jnp.dot(scr.att[second, :], v_ref[second, :].astype(BF16), preferred_element_type=F32)
                             + jnp.dot(scr.atx[second, :], v1, preferred_element_type=F32))


def _gla_pairs(nchunks, reverse):
    c = GLA_CHUNK
    order = range(nchunks // 2 - 1, -1, -1) if reverse else range(nchunks // 2)
    pairs = []
    for p in order:
        lo, hi = slice(2 * p * c, (2 * p + 1) * c), slice((2 * p + 1) * c, (2 * p + 2) * c)
        pairs.append((hi, lo) if reverse else (lo, hi))
    return pairs


def _gla_carry(first, second, v_ref, scr, state_ref):
    rows = slice(min(first.start, second.start), max(first.stop, second.stop))
    st = state_ref[...]
    o = jnp.dot(scr.qr[rows, :], st.astype(BF16), preferred_element_type=F32)
    decay = jnp.concatenate([scr.dec[rows.start // (2 * GLA_CHUNK)]] * (st.shape[1] // LANES), axis=1)
    state_ref[...] = st * decay + lax.dot_general(
        scr.ku[rows, :], v_ref[rows, :].astype(BF16), (((0,), (0,)), ((), ())), preferred_element_type=F32)
    return rows, o


_GlaScratch = collections.namedtuple("_GlaScratch", "gp b bt pq pk qd qr kd ke ku dec att atx")


def _gla_scratch_shapes(rows, dk):
    return _GlaScratch(
        gp=pltpu.VMEM((3, rows, dk), BF16),
        b=pltpu.VMEM((rows, dk), F32),
        bt=pltpu.VMEM((rows, dk), F32),
        pq=pltpu.VMEM((rows, dk), F32),
        pk=pltpu.VMEM((rows, dk), F32),
        qd=pltpu.VMEM((rows, dk), BF16),
        qr=pltpu.VMEM((rows, dk), BF16),
        kd=pltpu.VMEM((rows, dk), BF16),
        ke=pltpu.VMEM((rows, dk), BF16),
        ku=pltpu.VMEM((rows, dk), BF16),
        dec=pltpu.VMEM((rows // (2 * GLA_CHUNK), dk, LANES), F32),
        att=pltpu.VMEM((rows, GLA_CHUNK), BF16),
        atx=pltpu.VMEM((rows, GLA_CHUNK), BF16))


def _gla_fwd_kernel(q_ref, k_ref, v_ref, z_ref, wg_ref, bg_ref, o_ref, state_ref, *scratch, scale):
    scr = _GlaScratch(*scratch)

    @pl.when(pl.program_id(1) == 0)
    def _():
        state_ref[...] = jnp.zeros(state_ref.shape, F32)

    _gla_prepare(q_ref, k_ref, v_ref, z_ref, wg_ref, bg_ref, o_ref, scr, reverse=False, scale=scale)
    for first, second in _gla_pairs(q_ref.shape[0] // GLA_CHUNK, False):
        rows, o_inter = _gla_carry(first, second, v_ref, scr, state_ref)
        o_ref[rows, :] += o_inter


def _gla_bwd_kernel(q_ref, k_ref, v_ref, z_ref, wg_ref, bg_ref, of_ref, r_ref, hn_ref, o_ref, state_ref,
                    oi_scr, *scratch, scale):
    scr = _GlaScratch(*scratch)

    @pl.when(pl.program_id(1) == 0)
    def _():
        state_ref[...] = jnp.zeros(state_ref.shape, F32)

    _gla_prepare(q_ref, k_ref, v_ref, z_ref, wg_ref, bg_ref, oi_scr, scr, reverse=True, scale=scale)
    for first, second in _gla_pairs(q_ref.shape[0] // GLA_CHUNK, True):
        rows, o_inter = _gla_carry(first, second, v_ref, scr, state_ref)
        o = of_ref[rows, :] + (oi_scr[rows, :] + o_inter)
        ms = jnp.mean(o * o, axis=-1, keepdims=True)
        on = (o * lax.rsqrt(ms + EPS)) * hn_ref[...]
        r = r_ref[rows, :]
        o_ref[rows, :] = (on * (r * jax.nn.sigmoid(r))).astype(o_ref.dtype)


def _gla_direction(proj, z, wg_pad, bg, row0, seq, dk_total, dv_total, heads, *, reverse,
                   o_fwd=None, head_norm=None):
    dk, dv = dk_total // heads, dv_total // heads
    rows = _tile(seq, GLA_STEP_ROWS, GLA_CHUNK)
    nchunks = rows // GLA_CHUNK
    nb = seq // rows
    assert row0 % rows == 0 and (2 * dk_total) % dv == 0 and nchunks % 2 == 0
    rb0 = row0 // rows
    kc0 = dk_total // dk
    vc0 = (2 * dk_total) // dv
    rc0 = (2 * dk_total + dv_total) // dv
    if reverse:
        rblk = lambda i: rb0 + nb - 1 - i
        oblk = lambda i: nb - 1 - i
    else:
        rblk = lambda i: rb0 + i
        oblk = lambda i: i
    in_specs = [pl.BlockSpec((rows, dk), lambda h, i: (rblk(i), h)),
                pl.BlockSpec((rows, dk), lambda h, i: (rblk(i), kc0 + h)),
                pl.BlockSpec((rows, dv), lambda h, i: (rblk(i), vc0 + h)),
                pl.BlockSpec((rows, LANES), lambda h, i: (rblk(i), 0)),
                pl.BlockSpec((LANES, dk), lambda h, i: (0, h)),
                pl.BlockSpec((1, dk), lambda h, i: (0, h))]
    args = [proj, proj, proj, z, wg_pad, bg.reshape(1, dk_total)]
    scale = dk ** -0.5
    if reverse:
        in_specs += [pl.BlockSpec((rows, dv), lambda h, i: (oblk(i), h)),
                     pl.BlockSpec((rows, dv), lambda h, i: (rblk(i), rc0 + h)),
                     pl.BlockSpec((1, dv), lambda h, i: (0, 0))]
        args += [o_fwd, proj, head_norm.reshape(1, dv)]
        kern = functools.partial(_gla_bwd_kernel, scale=scale)
        out_dtype = BF16
    else:
        kern = functools.partial(_gla_fwd_kernel, scale=scale)
        out_dtype = F32
    scratch = [pltpu.VMEM((dk, dv), F32)]
    if reverse:
        scratch.append(pltpu.VMEM((rows, dv), F32))
    scratch += list(_gla_scratch_shapes(rows, dk))
    return pl.pallas_call(
        kern,
        grid=(heads, nb),
        in_specs=in_specs,
        out_specs=pl.BlockSpec((rows, dv), lambda h, i: (oblk(i), h)),
        out_shape=jax.ShapeDtypeStruct((seq, dv_total), out_dtype),
        scratch_shapes=scratch,
        compiler_params=_params("parallel", "arbitrary"),
        name="gla_bwd" if reverse else "gla_fwd",
    )(*args)


def kernel(x_prompt, x_sample, ffn_norm, ffn_w13, ffn_w2, mix_norm, attn_w_qkv, attn_q_norm, attn_k_norm, attn_w_o, gla_w_in, gla_w_gate_f, gla_b_gate_f, gla_w_gate_b, gla_b_gate_b, gla_head_norm, gla_w_o, final_norm):
    d_model = x_prompt.shape[-1]
    hd = attn_q_norm.shape[-1]
    depth = ffn_norm.shape[0]
    dk_total = gla_w_gate_f.shape[-1]
    rank = gla_w_gate_f.shape[1]
    dv_total = d_model
    gla_heads = dv_total // gla_head_norm.shape[-1]
    assert x_prompt.shape[0] == 1 and x_sample.shape[0] == 1 and 2 * rank <= LANES

    named = sorted([("sample", x_sample[0]), ("prompt", x_prompt[0])], key=lambda a: -a[1].shape[0])
    seqs = [s for _, s in named]
    lens = [s.shape[0] for s in seqs]
    starts = [sum(lens[:i]) for i in range(len(lens))]

    tables = [_rope_tables(n, hd) for n in lens]
    cos = jnp.concatenate([t[0] for t in tables], axis=0)
    sin = jnp.concatenate([t[1] for t in tables], axis=0)

    w13, w2 = ffn_w13, ffn_w2.astype(BF16)
    w_qkv, w_ao, w_go = attn_w_qkv, attn_w_o, gla_w_o
    x, nx = _prep(seqs[0], seqs[1], ffn_norm[0, 0])
    for i in range(depth):
        x, nx = _ffn(x, nx, w13, w2, (i, 0), mix_norm[i], rows_inner=True)
        j = i // 2
        if i % 2 == 0:
            qkv = _qkv_proj(nx, w_qkv, (j,), attn_q_norm[j], attn_k_norm[j], cos, sin, d_model, hd)
            o = jnp.concatenate([_flash_attention(qkv, r0, n, d_model, hd) for r0, n in zip(starts, lens)], axis=0)
            x, nx = _matmul_residual(o, w_ao, (j,), x, 1.0, ffn_norm[i, 1], tm=1024, tn=512, rows_inner=True)
        else:
            n_main = 2 * dk_total + 2 * dv_total
            w_z = jnp.zeros((d_model, LANES), BF16).at[:, :2 * rank].set(gla_w_in[j, :, n_main:].astype(BF16))
            proj, z = _gla_in_proj(nx, gla_w_in, (j,), n_main, w_z)
            wgf = jnp.zeros((LANES, dk_total), BF16).at[:rank].set(gla_w_gate_f[j].astype(BF16))
            wgb = jnp.zeros((LANES, dk_total), BF16).at[rank:2 * rank].set(gla_w_gate_b[j].astype(BF16))
            outs = []
            for r0, n in zip(starts, lens):
                o_f = _gla_direction(proj, z, wgf, gla_b_gate_f[j], r0, n, dk_total, dv_total, gla_heads,
                                     reverse=False)
                outs.append(_gla_direction(proj, z, wgb, gla_b_gate_b[j], r0, n, dk_total, dv_total, gla_heads,
                                           reverse=True, o_fwd=o_f, head_norm=gla_head_norm[j]))
            o = jnp.concatenate(outs, axis=0)
            x, nx = _matmul_residual(o, w_go, (j,), x, 1.0, ffn_norm[i, 1], tm=1024, tn=512, rows_inner=True)
        g_after = ffn_norm[i + 1, 0] if i + 1 < depth else None
        x, nx = _ffn(x, nx, w13, w2, (i, 1), g_after, rows_inner=True)

    outs = {name: _rmsnorm(x, final_norm, F32, r0, n)[None] for (name, _), r0, n in zip(named, starts, lens)}
    return (outs["prompt"], outs["sample"])
```

```python
import collections
import functools
import math

import jax
import jax.numpy as jnp
from jax import lax
from jax.experimental import pallas as pl
from jax.experimental.pallas import tpu as pltpu

GRID_W = 64
ROPE_THETA = 10000.0
GLA_CHUNK = 64
GLA_GATE_NORM = 16.0
FFN_HALF = 0.5
EPS = 1e-6
SOFTMAX_ROWS = 32
FLASH_UNROLL = 32
GLA_STEP_ROWS = 512
NORM_ROWS = 32
LOG2_E = 1.4426950408889634

LANES = 128
V7X_VMEM_LIMIT_BYTES = 60 * 1024 * 1024

F32 = jnp.float32
BF16 = jnp.bfloat16


def _tile(dim, pref, unit=LANES):
    if dim <= pref:
        return dim
    t = (pref // unit) * unit
    while t >= unit:
        if dim % t == 0:
            return t
        t -= unit
    return dim


def _params(*sem):
    return pltpu.CompilerParams(dimension_semantics=sem, vmem_limit_bytes=V7X_VMEM_LIMIT_BYTES)


def _rmsnorm_kernel(x_ref, g_ref, o_ref):
    x = x_ref[...]
    ms = jnp.mean(x * x, axis=-1, keepdims=True)
    o_ref[...] = ((x * lax.rsqrt(ms + EPS)) * g_ref[...]).astype(o_ref.dtype)


def _rmsnorm(x, g, out_dtype, row0, nrows):
    d = x.shape[1]
    tr = _tile(nrows, 256, 8)
    assert row0 % tr == 0
    rb0 = row0 // tr
    return pl.pallas_call(
        _rmsnorm_kernel,
        grid=(nrows // tr,),
        in_specs=[pl.BlockSpec((tr, d), lambda i: (rb0 + i, 0)),
                  pl.BlockSpec((1, d), lambda i: (0, 0))],
        out_specs=pl.BlockSpec((tr, d), lambda i: (i, 0)),
        out_shape=jax.ShapeDtypeStruct((nrows, d), out_dtype),
        compiler_params=_params("parallel"),
        name="rmsnorm",
    )(x, g.reshape(1, d))


def _fold_lanes(y):
    out = y[:, 0:LANES]
    for c in range(1, y.shape[1] // LANES):
        out = out + y[:, c * LANES:(c + 1) * LANES]
    return out


def _normalize_rows(xg_ref, ssq_ref, h_scr, r_scr):
    d = xg_ref.shape[1]
    ms = jnp.sum(ssq_ref[...], axis=-1, keepdims=True) * (1.0 / d)
    r_scr[...] = jnp.broadcast_to(lax.rsqrt(ms + EPS), r_scr.shape)

    def body(c, carry):
        rows = pl.ds(pl.multiple_of(c * NORM_ROWS, NORM_ROWS), NORM_ROWS)
        r = jnp.concatenate([r_scr[rows, :]] * (d // LANES), axis=1)
        h_scr[rows, :] = (xg_ref[rows, :].astype(F32) * r).astype(h_scr.dtype)
        return carry

    lax.fori_loop(0, xg_ref.shape[0] // NORM_ROWS, body, 0)


def _prep_kernel(xa_ref, xb_ref, g_ref, x_ref, xg_ref, ssq_ref, *, nb_a):
    def emit(src_ref):
        x = src_ref[...]
        x_ref[...] = x
        xg_ref[...] = (x * g_ref[...]).astype(xg_ref.dtype)
        ssq_ref[...] = _fold_lanes(x * x)

    @pl.when(pl.program_id(0) < nb_a)
    def _():
        emit(xa_ref)

    @pl.when(pl.program_id(0) >= nb_a)
    def _():
        emit(xb_ref)


def _prep(xa, xb, g):
    (sa, d), sb = xa.shape, xb.shape[0]
    m = sa + sb
    tr = _tile(math.gcd(sa, sb), 256, 8)
    nb_a = sa // tr
    outs = pl.pallas_call(
        functools.partial(_prep_kernel, nb_a=nb_a),
        grid=(m // tr,),
        in_specs=[pl.BlockSpec((tr, d), lambda i: (jnp.minimum(i, nb_a - 1), 0)),
                  pl.BlockSpec((tr, d), lambda i: (jnp.maximum(i - nb_a, 0), 0)),
                  pl.BlockSpec((1, d), lambda i: (0, 0))],
        out_specs=[pl.BlockSpec((tr, d), lambda i: (i, 0)),
                   pl.BlockSpec((tr, d), lambda i: (i, 0)),
                   pl.BlockSpec((tr, LANES), lambda i: (i, 0))],
        out_shape=[jax.ShapeDtypeStruct((m, d), F32), jax.ShapeDtypeStruct((m, d), BF16),
                   jax.ShapeDtypeStruct((m, LANES), F32)],
        compiler_params=_params("arbitrary"),
        name="prenorm_prep",
    )(xa, xb, g.reshape(1, d))
    return outs[0], (outs[1], outs[2])


def _weight_spec(w, idx, tn, col_block):
    k = w.shape[-2]
    return pl.BlockSpec((None,) * len(idx) + (k, tn), lambda i, j: (*idx, 0, col_block(i, j)))


def _mm_residual_kernel(a_ref, w_ref, r_ref, *rest, scale, emit_next, accumulate_ssq):
    acc = jnp.dot(a_ref[...], w_ref[...].astype(BF16), preferred_element_type=F32)
    xn = r_ref[...] + scale * acc
    if not emit_next:
        (o_ref,) = rest
        o_ref[...] = xn
        return
    gn_ref, o_ref, xg_ref, ssq_ref = rest
    o_ref[...] = xn
    xg_ref[...] = (xn * gn_ref[...]).astype(xg_ref.dtype)
    if accumulate_ssq:
        @pl.when(pl.program_id(1) == 0)
        def _():
            ssq_ref[...] = jnp.zeros(ssq_ref.shape, F32)

        ssq_ref[...] += _fold_lanes(xn * xn)
    else:
        ssq_ref[...] = _fold_lanes(xn * xn)


def _matmul_residual(a, w, w_idx, res, scale, g_next, *, tm, tn, rows_inner):
    m, k = a.shape
    n = w.shape[-1]
    tm, tn = _tile(m, tm, 8), _tile(n, tn)
    emit_next = g_next is not None

    def spec(shape, index):
        return pl.BlockSpec(shape, (lambda j, i: index(i, j)) if rows_inner else index)

    in_specs = [spec((tm, k), lambda i, j: (i, 0)),
                spec((None,) * len(w_idx) + (k, tn), lambda i, j: (*w_idx, 0, j)),
                spec((tm, tn), lambda i, j: (i, j))]
    out_specs = [spec((tm, tn), lambda i, j: (i, j))]
    out_shape = [jax.ShapeDtypeStruct((m, n), F32)]
    args = [a, w, res]
    if emit_next:
        ssq_groups = n // tn if rows_inner else 1
        in_specs.append(spec((1, tn), lambda i, j: (0, j)))
        out_specs += [spec((tm, tn), lambda i, j: (i, j)),
                      spec((tm, LANES), (lambda i, j: (i, j)) if rows_inner else (lambda i, j: (i, 0)))]
        out_shape += [jax.ShapeDtypeStruct((m, n), BF16), jax.ShapeDtypeStruct((m, ssq_groups * LANES), F32)]
        args.append(g_next.reshape(1, n))
    outs = pl.pallas_call(
        functools.partial(_mm_residual_kernel, scale=scale, emit_next=emit_next, accumulate_ssq=not rows_inner),
        grid=(n // tn, m // tm) if rows_inner else (m // tm, n // tn),
        in_specs=in_specs,
        out_specs=out_specs,
        out_shape=out_shape,
        compiler_params=_params("parallel", "arbitrary"),
        name="matmul_residual",
    )(*args)
    return (outs[0], (outs[1], outs[2])) if emit_next else (outs[0], None)


def _ffn_up_kernel(xg_ref, ssq_ref, wa_ref, wb_ref, o_ref, h_scr, r_scr):
    @pl.when(pl.program_id(1) == 0)
    def _():
        _normalize_rows(xg_ref, ssq_ref, h_scr, r_scr)

    h = h_scr[...]
    a = jnp.dot(h, wa_ref[...].astype(BF16), preferred_element_type=F32)
    b = jnp.dot(h, wb_ref[...].astype(BF16), preferred_element_type=F32)
    o_ref[...] = ((a * jax.nn.sigmoid(a)) * b).astype(o_ref.dtype)


def _ffn_up(nx, w13, w_idx):
    xg, ssq = nx
    m, k = xg.shape
    f = w13.shape[-1] // 2
    tm, tn = _tile(m, 1024, 8), _tile(f, 256)
    nb = f // tn
    return pl.pallas_call(
        _ffn_up_kernel,
        grid=(m // tm, nb),
        in_specs=[pl.BlockSpec((tm, k), lambda i, j: (i, 0)),
                  pl.BlockSpec((tm, ssq.shape[1]), lambda i, j: (i, 0)),
                  _weight_spec(w13, w_idx, tn, lambda i, j: j),
                  _weight_spec(w13, w_idx, tn, lambda i, j: j + nb)],
        out_specs=pl.BlockSpec((tm, tn), lambda i, j: (i, j)),
        out_shape=jax.ShapeDtypeStruct((m, f), BF16),
        scratch_shapes=[pltpu.VMEM((tm, k), BF16), pltpu.VMEM((tm, LANES), F32)],
        compiler_params=_params("parallel", "arbitrary"),
        name="ffn_up",
    )(xg, ssq, w13, w13)


def _ffn(x, nx, w13, w2, w_idx, g_next, rows_inner):
    u = _ffn_up(nx, w13, w_idx)
    return _matmul_residual(u, w2, w_idx, x, FFN_HALF, g_next, tm=512, tn=512, rows_inner=rows_inner)


def _rope_tables(seq_len, head_dim):
    axis = head_dim // 2
    pos = jnp.arange(seq_len, dtype=jnp.int32)
    row_ids = (pos // GRID_W).astype(F32)
    col_ids = (pos % GRID_W).astype(F32)
    inv_freq = ROPE_THETA ** (-jnp.arange(0, axis, 2, dtype=F32) / axis)
    ang_r = row_ids[:, None] * inv_freq[None, :]
    ang_c = col_ids[:, None] * inv_freq[None, :]
    cos = jnp.concatenate([jnp.cos(ang_r), jnp.cos(ang_r), jnp.cos(ang_c), jnp.cos(ang_c)], axis=-1)
    sin = jnp.concatenate([-jnp.sin(ang_r), jnp.sin(ang_r), -jnp.sin(ang_c), jnp.sin(ang_c)], axis=-1)
    return cos, sin


def _qkv_kernel(xg_ref, ssq_ref, w_ref, gain_ref, cos_ref, sin_ref, o_ref, h_scr, r_scr,
                *, hd, n_q_blocks, n_qk_blocks, q_scale):
    j = pl.program_id(1)

    @pl.when(j == 0)
    def _():
        _normalize_rows(xg_ref, ssq_ref, h_scr, r_scr)

    acc = jnp.dot(h_scr[...], w_ref[...].astype(BF16), preferred_element_type=F32)
    heads = acc.shape[1] // hd
    quarter = hd // 4

    def norm_rope(post_scale):
        cos = cos_ref[...]
        sin = sin_ref[...]
        lane = lax.broadcasted_iota(jnp.int32, cos.shape, 1)
        first_half = (lane % (2 * quarter)) < quarter
        for hh in range(heads):
            x = acc[:, hh * hd:(hh + 1) * hd]
            ms = jnp.mean(x * x, axis=-1, keepdims=True)
            y = (x * lax.rsqrt(ms + EPS)) * gain_ref[:, hh * hd:(hh + 1) * hd]
            partner = jnp.where(first_half, pltpu.roll(y, hd - quarter, 1), pltpu.roll(y, quarter, 1))
            out = y * cos + partner * sin
            if post_scale != 1.0:
                out = out * post_scale
            o_ref[:, hh * hd:(hh + 1) * hd] = out.astype(o_ref.dtype)

    @pl.when(j < n_q_blocks)
    def _():
        norm_rope(q_scale)

    @pl.when(jnp.logical_and(j >= n_q_blocks, j < n_qk_blocks))
    def _():
        norm_rope(1.0)

    @pl.when(j >= n_qk_blocks)
    def _():
        o_ref[...] = acc.astype(o_ref.dtype)


def _qkv_proj(nx, w_qkv, w_idx, q_norm, k_norm, cos, sin, d_model, hd):
    xg, ssq = nx
    m, k = xg.shape
    n = w_qkv.shape[-1]
    kvd = (n - d_model) // 2
    tm = _tile(m, 1024, 8)
    tn = _tile(kvd, 512)
    assert d_model % tn == 0 and tn % hd == 0
    gain = jnp.concatenate([jnp.tile(q_norm, d_model // hd), jnp.tile(k_norm, kvd // hd),
                            jnp.ones((kvd,), F32)]).reshape(1, n)
    kern = functools.partial(_qkv_kernel, hd=hd, n_q_blocks=d_model // tn,
                             n_qk_blocks=(d_model + kvd) // tn, q_scale=hd ** -0.5 * LOG2_E)
    return pl.pallas_call(
        kern,
        grid=(m // tm, n // tn),
        in_specs=[pl.BlockSpec((tm, k), lambda i, j: (i, 0)),
                  pl.BlockSpec((tm, ssq.shape[1]), lambda i, j: (i, 0)),
                  _weight_spec(w_qkv, w_idx, tn, lambda i, j: j),
                  pl.BlockSpec((1, tn), lambda i, j: (0, j)),
                  pl.BlockSpec((tm, hd), lambda i, j: (i, 0)),
                  pl.BlockSpec((tm, hd), lambda i, j: (i, 0))],
        out_specs=pl.BlockSpec((tm, tn), lambda i, j: (i, j)),
        out_shape=jax.ShapeDtypeStruct((m, n), BF16),
        scratch_shapes=[pltpu.VMEM((tm, k), BF16), pltpu.VMEM((tm, LANES), F32)],
        compiler_params=_params("parallel", "arbitrary"),
        name="qkv_proj",
    )(xg, ssq, w_qkv, gain, cos, sin)


def _flash_kernel(q_ref, k_ref, v_ref, *rest, group, hd, tk, unroll, shared_out):
    o_ref, m_scr, acc_scr, a_scr, p_scr, s_a, s_b = rest[1:] if shared_out else rest
    _flash_body(q_ref, k_ref, v_ref, o_ref, m_scr, acc_scr, a_scr, p_scr, s_a, s_b,
                group=group, hd=hd, tk=tk, unroll=unroll)


def _flash_body(q_ref, k_ref, v_ref, o_ref, m_scr, acc_scr, a_scr, p_scr, s_a, s_b,
                *, group, hd, tk, unroll):
    tq = q_ref.shape[0]
    nk = k_ref.shape[0] // tk
    q = jnp.concatenate([q_ref[:, g * hd:(g + 1) * hd] for g in range(group)], axis=0)
    m_scr[...] = jnp.full(m_scr.shape, -jnp.inf, F32)
    acc_scr[...] = jnp.zeros(acc_scr.shape, F32)

    bufs = (s_a, s_b)

    def scores(c, s_out):
        r0 = pl.multiple_of(c * tk, tk)
        s_out[...] = lax.dot_general(q, k_ref[pl.ds(r0, tk), :], (((1,), (1,)), ((), ())),
                                     preferred_element_type=F32)

    def update(c, s_in):
        r0 = pl.multiple_of(c * tk, tk)
        for rb in range(0, group * tq, SOFTMAX_ROWS):
            rows = slice(rb, rb + SOFTMAX_ROWS)
            s = s_in[rows, :]
            m_prev = m_scr[rows, :]
            m_new = jnp.maximum(m_prev, jnp.max(s, axis=-1, keepdims=True))
            a_scr[rows, :] = jnp.exp2(m_prev - m_new)
            m_scr[rows, :] = m_new
            p_scr[rows, :] = jnp.exp2(s - jnp.concatenate([m_new] * (tk // LANES), axis=1)).astype(BF16)
        v_ext = jnp.concatenate([v_ref[pl.ds(r0, tk), :], jnp.ones((tk, LANES), BF16)], axis=1)
        alpha = a_scr[...]
        acc_scr[...] = (jnp.concatenate([alpha, alpha], axis=1) * acc_scr[...]
                        + jnp.dot(p_scr[...], v_ext, preferred_element_type=F32))

    scores(0, s_a)

    def steady(jj, carry):
        c0 = unroll * jj
        for u in range(unroll):
            scores(c0 + u + 1, bufs[(u + 1) % 2])
            update(c0 + u, bufs[u % 2])
        return carry

    lax.fori_loop(0, nk // unroll - 1, steady, 0)
    c0 = nk - unroll
    for u in range(unroll):
        if u + 1 < unroll:
            scores(c0 + u + 1, bufs[(u + 1) % 2])
        update(c0 + u, bufs[u % 2])
    out = acc_scr[:, 0:hd] / acc_scr[:, hd:2 * hd]
    for g in range(group):
        o_ref[:, g * hd:(g + 1) * hd] = out[g * tq:(g + 1) * tq].astype(o_ref.dtype)


def _flash_attention(qkv, row0, seq, d_model, hd, out_buf=None):
    n = qkv.shape[1]
    kvd = (n - d_model) // 2
    n_kv = kvd // hd
    group = d_model // kvd
    tq = _tile(seq, 256, 8)
    tk = _tile(seq // 2, 512)
    nk = seq // tk
    unroll = max(u for u in range(2, FLASH_UNROLL + 1, 2) if nk % u == 0)
    rows = group * tq
    assert hd == LANES and row0 % seq == 0 and row0 % tq == 0 and nk % unroll == 0
    assert rows % SOFTMAX_ROWS == 0
    qb0, sb0 = row0 // tq, row0 // seq
    kc0, vc0 = d_model // hd, (d_model + kvd) // hd
    shared = out_buf is not None
    kern = functools.partial(_flash_kernel, group=group, hd=hd, tk=tk, unroll=unroll, shared_out=shared)
    in_specs = [pl.BlockSpec((tq, group * hd), lambda h, i: (qb0 + i, h)),
                pl.BlockSpec((seq, hd), lambda h, i: (sb0, kc0 + h)),
                pl.BlockSpec((seq, hd), lambda h, i: (sb0, vc0 + h))]
    args = [qkv, qkv, qkv]
    if shared:
        in_specs.append(pl.BlockSpec(memory_space=pl.ANY))
        args.append(out_buf)
    return pl.pallas_call(
        kern,
        grid=(n_kv, seq // tq),
        in_specs=in_specs,
        out_specs=pl.BlockSpec((tq, group * hd), lambda h, i: (qb0 + i, h)),
        out_shape=jax.ShapeDtypeStruct((qkv.shape[0], d_model), BF16),
        input_output_aliases={3: 0} if shared else {},
        scratch_shapes=[pltpu.VMEM((rows, LANES), F32),
                        pltpu.VMEM((rows, 2 * hd), F32),
                        pltpu.VMEM((rows, LANES), F32),
                        pltpu.VMEM((rows, tk), BF16),
                        pltpu.VMEM((rows, tk), F32),
                        pltpu.VMEM((rows, tk), F32)],
        compiler_params=_params("parallel", "arbitrary"),
        name="flash_attention",
    )(*args)


def _gla_in_kernel(xg_ref, ssq_ref, w_ref, wz_ref, o_ref, z_ref, h_scr, r_scr):
    @pl.when(pl.program_id(1) == 0)
    def _():
        _normalize_rows(xg_ref, ssq_ref, h_scr, r_scr)
        z_ref[...] = jnp.dot(h_scr[...], wz_ref[...], preferred_element_type=F32)

    o_ref[...] = jnp.dot(h_scr[...], w_ref[...].astype(BF16), preferred_element_type=F32)


def _gla_in_proj(nx, w_in, w_idx, n, w_z):
    xg, ssq = nx
    m, k = xg.shape
    tm, tn = _tile(m, 1024, 8), _tile(n, 512)
    return pl.pallas_call(
        _gla_in_kernel,
        grid=(m // tm, n // tn),
        in_specs=[pl.BlockSpec((tm, k), lambda i, j: (i, 0)),
                  pl.BlockSpec((tm, ssq.shape[1]), lambda i, j: (i, 0)),
                  _weight_spec(w_in, w_idx, tn, lambda i, j: j),
                  pl.BlockSpec((k, LANES), lambda i, j: (0, 0))],
        out_specs=[pl.BlockSpec((tm, tn), lambda i, j: (i, j)),
                   pl.BlockSpec((tm, LANES), lambda i, j: (i, 0))],
        out_shape=[jax.ShapeDtypeStruct((m, n), F32), jax.ShapeDtypeStruct((m, LANES), F32)],
        scratch_shapes=[pltpu.VMEM((tm, k), BF16), pltpu.VMEM((tm, LANES), F32)],
        compiler_params=_params("parallel", "arbitrary"),
        name="gla_in_proj",
    )(xg, ssq, w_in, w_z)


def _log_sigmoid(x):
    return jnp.minimum(x, 0.0) - jnp.log1p(jnp.exp(-jnp.abs(x)))


def _gla_prepare(q_ref, k_ref, v_ref, z_ref, wg_ref, bg_ref, oi_ref, scr, *, reverse, scale):
    c = GLA_CHUNK
    dk = q_ref.shape[1]
    pre = jnp.dot(z_ref[...].astype(BF16), wg_ref[...], preferred_element_type=F32) + bg_ref[...]
    g = _log_sigmoid(pre) / GLA_GATE_NORM
    g1 = g.astype(BF16)
    r1 = g - g1.astype(F32)
    g2 = r1.astype(BF16)
    scr.gp[0] = g1
    scr.gp[1] = g2
    scr.gp[2] = (r1 - g2.astype(F32)).astype(BF16)

    row4 = lax.broadcasted_iota(jnp.int32, (c, 4 * c), 0)
    col4 = lax.broadcasted_iota(jnp.int32, (c, 4 * c), 1)
    src = lax.rem(col4, c)
    tri4 = jnp.logical_and((row4 <= src) if reverse else (row4 >= src), col4 < 3 * c).astype(BF16)
    zeros = jnp.zeros((c, dk), BF16)
    zeros_f32 = jnp.zeros((c, dk), F32)

    def cumulative(rows):
        b = jnp.dot(tri4, jnp.concatenate([scr.gp[0, rows, :], scr.gp[1, rows, :], scr.gp[2, rows, :], zeros],
                                          axis=0), preferred_element_type=F32)
        return b, (b[0:1, :] if reverse else b[c - 1:c, :])

    for first, second in _gla_pairs(q_ref.shape[0] // c, reverse):
        b1, t1 = cumulative(first)
        b2, t2 = cumulative(second)
        scr.b[first, :] = b1
        scr.b[second, :] = b2
        scr.bt[first, :] = jnp.broadcast_to(t1, (c, dk))
        scr.bt[second, :] = jnp.broadcast_to(t2, (c, dk))
        scr.pq[first, :] = zeros_f32
        scr.pq[second, :] = jnp.broadcast_to(t1, (c, dk))
        scr.pk[first, :] = jnp.broadcast_to(t2, (c, dk))
        scr.pk[second, :] = zeros_f32
        scr.dec[min(first.start, second.start) // (2 * c)] = jnp.exp(
            jnp.transpose(jnp.broadcast_to(t1 + t2, (LANES, dk))))

    b = scr.b[...]
    k = k_ref[...]
    qs = q_ref[...] * scale
    scr.qd[...] = (qs * jnp.exp(b)).astype(BF16)
    scr.qr[...] = (qs * jnp.exp(b + scr.pq[...])).astype(BF16)
    scr.kd[...] = (k * jnp.exp(-b)).astype(BF16)
    tail = scr.bt[...] - b
    scr.ke[...] = (k * jnp.exp(tail)).astype(BF16)
    scr.ku[...] = (k * jnp.exp(tail + scr.pk[...])).astype(BF16)

    row = lax.broadcasted_iota(jnp.int32, (c, c), 0)
    col = lax.broadcasted_iota(jnp.int32, (c, c), 1)
    tri = (row <= col) if reverse else (row >= col)
    nt = (((1,), (1,)), ((), ()))
    for first, second in _gla_pairs(q_ref.shape[0] // c, reverse):
        for rows in (first, second):
            att = lax.dot_general(scr.qd[rows, :], scr.kd[rows, :], nt, preferred_element_type=F32)
            scr.att[rows, :] = jnp.where(tri, att, 0.0).astype(BF16)
        scr.atx[second, :] = lax.dot_general(scr.qd[second, :], scr.ke[first, :], nt,
                                             preferred_element_type=F32).astype(BF16)
    for first, second in _gla_pairs(q_ref.shape[0] // c, reverse):
        v1 = v_ref[first, :].astype(BF16)
        oi_ref[first, :] = jnp.dot(scr.att[first, :], v1, preferred_element_type=F32)
        oi_ref[second, :] = (jnp.dot(scr.att[second, :], v_ref[second, :].astype(BF16), preferred_element_type=F32)
                             + jnp.dot(scr.atx[second, :], v1, preferred_element_type=F32))


def _gla_pairs(nchunks, reverse):
    c = GLA_CHUNK
    order = range(nchunks // 2 - 1, -1, -1) if reverse else range(nchunks // 2)
    pairs = []
    for p in order:
        lo, hi = slice(2 * p * c, (2 * p + 1) * c), slice((2 * p + 1) * c, (2 * p + 2) * c)
        pairs.append((hi, lo) if reverse else (lo, hi))
    return pairs


def _gla_carry(first, second, v_ref, scr, state_ref):
    rows = slice(min(first.start, second.start), max(first.stop, second.stop))
    st = state_ref[...]
    o = jnp.dot(scr.qr[rows, :], st.astype(BF16), preferred_element_type=F32)
    decay = jnp.concatenate([scr.dec[rows.start // (2 * GLA_CHUNK)]] * (st.shape[1] // LANES), axis=1)
    state_ref[...] = st * decay + lax.dot_general(
        scr.ku[rows, :], v_ref[rows, :].astype(BF16), (((0,), (0,)), ((), ())), preferred_element_type=F32)
    return rows, o


_GlaScratch = collections.namedtuple("_GlaScratch", "gp b bt pq pk qd qr kd ke ku dec att atx")


def _gla_scratch_shapes(rows, dk):
    return _GlaScratch(
        gp=pltpu.VMEM((3, rows, dk), BF16),
        b=pltpu.VMEM((rows, dk), F32),
        bt=pltpu.VMEM((rows, dk), F32),
        pq=pltpu.VMEM((rows, dk), F32),
        pk=pltpu.VMEM((rows, dk), F32),
        qd=pltpu.VMEM((rows, dk), BF16),
        qr=pltpu.VMEM((rows, dk), BF16),
        kd=pltpu.VMEM((rows, dk), BF16),
        ke=pltpu.VMEM((rows, dk), BF16),
        ku=pltpu.VMEM((rows, dk), BF16),
        dec=pltpu.VMEM((rows // (2 * GLA_CHUNK), dk, LANES), F32),
        att=pltpu.VMEM((rows, GLA_CHUNK), BF16),
        atx=pltpu.VMEM((rows, GLA_CHUNK), BF16))


def _gla_fwd_kernel(q_ref, k_ref, v_ref, z_ref, wg_ref, bg_ref, o_ref, state_ref, *scratch, scale):
    scr = _GlaScratch(*scratch)

    @pl.when(pl.program_id(1) == 0)
    def _():
        state_ref[...] = jnp.zeros(state_ref.shape, F32)

    _gla_prepare(q_ref, k_ref, v_ref, z_ref, wg_ref, bg_ref, o_ref, scr, reverse=False, scale=scale)
    for first, second in _gla_pairs(q_ref.shape[0] // GLA_CHUNK, False):
        rows, o_inter = _gla_carry(first, second, v_ref, scr, state_ref)
        o_ref[rows, :] += o_inter


def _gla_bwd_kernel(q_ref, k_ref, v_ref, z_ref, wg_ref, bg_ref, of_ref, r_ref, hn_ref, o_ref, state_ref,
                    oi_scr, *scratch, scale):
    scr = _GlaScratch(*scratch)

    @pl.when(pl.program_id(1) == 0)
    def _():
        state_ref[...] = jnp.zeros(state_ref.shape, F32)

    _gla_prepare(q_ref, k_ref, v_ref, z_ref, wg_ref, bg_ref, oi_scr, scr, reverse=True, scale=scale)
    for first, second in _gla_pairs(q_ref.shape[0] // GLA_CHUNK, True):
        rows, o_inter = _gla_carry(first, second, v_ref, scr, state_ref)
        o = of_ref[rows, :] + (oi_scr[rows, :] + o_inter)
        ms = jnp.mean(o * o, axis=-1, keepdims=True)
        on = (o * lax.rsqrt(ms + EPS)) * hn_ref[...]
        r = r_ref[rows, :]
        o_ref[rows, :] = (on * (r * jax.nn.sigmoid(r))).astype(o_ref.dtype)


def _gla_direction(proj, z, wg_pad, bg, row0, seq, dk_total, dv_total, heads, *, reverse,
                   o_fwd=None, head_norm=None):
    dk, dv = dk_total // heads, dv_total // heads
    rows = _tile(seq, GLA_STEP_ROWS, GLA_CHUNK)
    nchunks = rows // GLA_CHUNK
    nb = seq // rows
    assert row0 % rows == 0 and (2 * dk_total) % dv == 0 and nchunks % 2 == 0
    rb0 = row0 // rows
    kc0 = dk_total // dk
    vc0 = (2 * dk_total) // dv
    rc0 = (2 * dk_total + dv_total) // dv
    if reverse:
        rblk = lambda i: rb0 + nb - 1 - i
        oblk = lambda i: nb - 1 - i
    else:
        rblk = lambda i: rb0 + i
        oblk = lambda i: i
    in_specs = [pl.BlockSpec((rows, dk), lambda h, i: (rblk(i), h)),
                pl.BlockSpec((rows, dk), lambda h, i: (rblk(i), kc0 + h)),
                pl.BlockSpec((rows, dv), lambda h, i: (rblk(i), vc0 + h)),
                pl.BlockSpec((rows, LANES), lambda h, i: (rblk(i), 0)),
                pl.BlockSpec((LANES, dk), lambda h, i: (0, h)),
                pl.BlockSpec((1, dk), lambda h, i: (0, h))]
    args = [proj, proj, proj, z, wg_pad, bg.reshape(1, dk_total)]
    scale = dk ** -0.5
    if reverse:
        in_specs += [pl.BlockSpec((rows, dv), lambda h, i: (oblk(i), h)),
                     pl.BlockSpec((rows, dv), lambda h, i: (rblk(i), rc0 + h)),
                     pl.BlockSpec((1, dv), lambda h, i: (0, 0))]
        args += [o_fwd, proj, head_norm.reshape(1, dv)]
        kern = functools.partial(_gla_bwd_kernel, scale=scale)
        out_dtype = BF16
    else:
        kern = functools.partial(_gla_fwd_kernel, scale=scale)
        out_dtype = F32
    scratch = [pltpu.VMEM((dk, dv), F32)]
    if reverse:
        scratch.append(pltpu.VMEM((rows, dv), F32))
    scratch += list(_gla_scratch_shapes(rows, dk))
    return pl.pallas_call(
        kern,
        grid=(heads, nb),
        in_specs=in_specs,
        out_specs=pl.BlockSpec((rows, dv), lambda h, i: (oblk(i), h)),
        out_shape=jax.ShapeDtypeStruct((seq, dv_total), out_dtype),
        scratch_shapes=scratch,
        compiler_params=_params("parallel", "arbitrary"),
        name="gla_bwd" if reverse else "gla_fwd",
    )(*args)


def kernel(x_prompt, x_sample, ffn_norm, ffn_w13, ffn_w2, mix_norm, attn_w_qkv, attn_q_norm, attn_k_norm, attn_w_o, gla_w_in, gla_w_gate_f, gla_b_gate_f, gla_w_gate_b, gla_b_gate_b, gla_head_norm, gla_w_o, final_norm):
    d_model = x_prompt.shape[-1]
    hd = attn_q_norm.shape[-1]
    depth = ffn_norm.shape[0]
    dk_total = gla_w_gate_f.shape[-1]
    rank = gla_w_gate_f.shape[1]
    dv_total = d_model
    gla_heads = dv_total // gla_head_norm.shape[-1]
    assert x_prompt.shape[0] == 1 and x_sample.shape[0] == 1 and 2 * rank <= LANES

    named = sorted([("sample", x_sample[0]), ("prompt", x_prompt[0])], key=lambda a: -a[1].shape[0])
    seqs = [s for _, s in named]
    lens = [s.shape[0] for s in seqs]
    starts = [sum(lens[:i]) for i in range(len(lens))]

    tables = [_rope_tables(n, hd) for n in lens]
    cos = jnp.concatenate([t[0] for t in tables], axis=0)
    sin = jnp.concatenate([t[1] for t in tables], axis=0)

    w13, w2 = ffn_w13, ffn_w2.astype(BF16)
    w_qkv, w_ao, w_go = attn_w_qkv, attn_w_o, gla_w_o
    x, nx = _prep(seqs[0], seqs[1], ffn_norm[0, 0])
    for i in range(depth):
        x, nx = _ffn(x, nx, w13, w2, (i, 0), mix_norm[i], rows_inner=True)
        j = i // 2
        if i % 2 == 0:
            qkv = _qkv_proj(nx, w_qkv, (j,), attn_q_norm[j], attn_k_norm[j], cos, sin, d_model, hd)
            o = None
            for r0, n in zip(starts, lens):
                o = _flash_attention(qkv, r0, n, d_model, hd, out_buf=o)
            x, nx = _matmul_residual(o, w_ao, (j,), x, 1.0, ffn_norm[i, 1], tm=1024, tn=512, rows_inner=True)
        else:
            n_main = 2 * dk_total + 2 * dv_total
            w_z = jnp.zeros((d_model, LANES), BF16).at[:, :2 * rank].set(gla_w_in[j, :, n_main:].astype(BF16))
            proj, z = _gla_in_proj(nx, gla_w_in, (j,), n_main, w_z)
            wgf = jnp.zeros((LANES, dk_total), BF16).at[:rank].set(gla_w_gate_f[j].astype(BF16))
            wgb = jnp.zeros((LANES, dk_total), BF16).at[rank:2 * rank].set(gla_w_gate_b[j].astype(BF16))
            outs = []
            for r0, n in zip(starts, lens):
                o_f = _gla_direction(proj, z, wgf, gla_b_gate_f[j], r0, n, dk_total, dv_total, gla_heads,
                                     reverse=False)
                outs.append(_gla_direction(proj, z, wgb, gla_b_gate_b[j], r0, n, dk_total, dv_total, gla_heads,
                                           reverse=True, o_fwd=o_f, head_norm=gla_head_norm[j]))
            o = jnp.concatenate(outs, axis=0)
            x, nx = _matmul_residual(o, w_go, (j,), x, 1.0, ffn_norm[i, 1], tm=1024, tn=512, rows_inner=True)
        g_after = ffn_norm[i + 1, 0] if i + 1 < depth else None
        x, nx = _ffn(x, nx, w13, w2, (i, 1), g_after, rows_inner=True)

    outs = {name: _rmsnorm(x, final_norm, F32, r0, n)[None] for (name, _), r0, n in zip(named, starts, lens)}
    return (outs["prompt"], outs["sample"])
```
